```python
import math
import jax, jax.numpy as jnp
from jax import lax
import numpy as np

D_MODEL = 1024
BATCH = 2
SEQ = 8192
DEPTH = 1

MIX_WIDTH = D_MODEL
ATTN_WIDTH = MIX_WIDTH // 2
SSM_WIDTH = MIX_WIDTH - ATTN_WIDTH
HEAD_DIM = 64
N_Q_HEADS = ATTN_WIDTH // HEAD_DIM
N_KV_HEADS = 2
KV_GROUP = N_Q_HEADS // N_KV_HEADS
WINDOW = 128
BLOCK = 128
ROPE_THETA = 500000.0
ROT_DIM = HEAD_DIM // 4
SSM_GROUP_CH = 16
N_SSM_GROUPS = SSM_WIDTH // SSM_GROUP_CH
SSM_STATE = 64
DT_MIN = 0.001
DT_MAX = 0.1
D_FF = 128 * ((8 * D_MODEL // 3 + 127) // 128)
CONV_WIDTH = 3
NORM_EPS = 1e-5
Q_COLS = N_Q_HEADS * HEAD_DIM
KV_COLS = N_KV_HEADS * HEAD_DIM
IN_COLS = Q_COLS + 2 * KV_COLS + SSM_WIDTH
MASK_VALUE = -1e30

kernel_name = "hybrid_swa_sink_s5_convffn"


def rms_norm(x, g):
    xf = x.astype(jnp.float32)
    y = xf * lax.rsqrt(jnp.mean(xf * xf, axis=-1, keepdims=True) + NORM_EPS)
    return (y * g.astype(jnp.float32)).astype(x.dtype)


def partial_rotary(x, pos):
    half = ROT_DIM // 2
    inv_freq = ROPE_THETA ** (-jnp.arange(half, dtype=jnp.float32) * 2.0 / ROT_DIM)
    ang = pos[:, None] * inv_freq[None, :]
    cos = jnp.cos(ang)[None, :, None, :]
    sin = jnp.sin(ang)[None, :, None, :]
    xf = x.astype(jnp.float32)
    x1 = xf[..., :half]
    x2 = xf[..., half:ROT_DIM]
    out = jnp.concatenate([x1 * cos - x2 * sin, x2 * cos + x1 * sin, xf[..., ROT_DIM:]], axis=-1)
    return out.astype(x.dtype)


def sliding_window_gqa(q, k, v, sinks):
    b, l = q.shape[0], q.shape[1]
    nb = l // BLOCK
    qb = q.astype(jnp.float32).reshape(b, nb, BLOCK, N_KV_HEADS, KV_GROUP, HEAD_DIM)
    kb = k.astype(jnp.float32).reshape(b, nb, BLOCK, N_KV_HEADS, HEAD_DIM)
    vb = v.astype(jnp.float32).reshape(b, nb, BLOCK, N_KV_HEADS, HEAD_DIM)
    shift = lambda t: jnp.pad(t, ((0, 0), (1, 0), (0, 0), (0, 0), (0, 0)))[:, :-1]
    kk = jnp.concatenate([shift(kb), kb], axis=2)
    vv = jnp.concatenate([shift(vb), vb], axis=2)
    s = jnp.einsum('bnqhgd,bnkhd->bnhgqk', qb, kk) * (HEAD_DIM ** -0.5)
    qi = jnp.arange(BLOCK)[:, None]
    kj = jnp.arange(2 * BLOCK)[None, :]
    diff = qi + BLOCK - kj
    band = (diff >= 0) & (diff < WINDOW)
    has_prev = (jnp.arange(nb)[:, None] > 0) | (kj >= BLOCK)
    mask = band[None, :, :] & has_prev[:, None, :]
    s = jnp.where(mask[None, :, None, None, :, :], s, MASK_VALUE)
    sink = sinks.astype(jnp.float32).reshape(N_KV_HEADS, KV_GROUP)[None, None, :, :, None, None]
    m = jnp.maximum(jnp.max(s, axis=-1, keepdims=True), sink)
    p = jnp.exp(s - m)
    denom = jnp.sum(p, axis=-1, keepdims=True) + jnp.exp(sink - m)
    o = jnp.einsum('bnhgqk,bnkhd->bnqhgd', p / denom, vv)
    return o.reshape(b, l, N_Q_HEADS * HEAD_DIM).astype(q.dtype)


def s5_ssm(u, lam_re, lam_im, log_step, b_re, b_im, c_re, c_im, d_skip):
    b, l = u.shape[0], u.shape[1]
    uf = u.astype(jnp.float32).reshape(b, l, N_SSM_GROUPS, SSM_GROUP_CH)
    lam = lax.complex(lam_re.astype(jnp.float32), lam_im.astype(jnp.float32))
    dt = jnp.exp(log_step.astype(jnp.float32))[:, None]
    lam_bar = jnp.exp(lam * dt)
    b_c = lax.complex(b_re.astype(jnp.float32), b_im.astype(jnp.float32))
    c_c = lax.complex(c_re.astype(jnp.float32), c_im.astype(jnp.float32))
    b_bar = ((lam_bar - 1.0) / lam)[..., None] * b_c
    bu = jnp.einsum('gpc,blgc->blgp', b_bar, uf.astype(jnp.complex64))
    a = jnp.broadcast_to(lam_bar, (1, l, N_SSM_GROUPS, SSM_STATE))

    def combine(e_i, e_j):
        a_i, s_i = e_i
        a_j, s_j = e_j
        return a_j * a_i, a_j * s_i + s_j

    _, states = lax.associative_scan(combine, (a, bu), axis=1)
    y = jnp.einsum('gcp,blgp->blgc', c_c, states).real + d_skip.astype(jnp.float32) * uf
    return y.reshape(b, l, SSM_WIDTH).astype(u.dtype)


def conv_ffn(h, w_up, conv_w, conv_b, w_down):
    up = h @ w_up
    up = lax.conv_general_dilated(
        up, conv_w[:, None, :], window_strides=(1,), padding=[(CONV_WIDTH - 1, 0)],
        dimension_numbers=('NWC', 'WIO', 'NWC'), feature_group_count=2 * D_FF) + conv_b
    gate, val = up[..., :D_FF], up[..., D_FF:]
    return (jax.nn.silu(gate) * val) @ w_down


def setup_inputs(seed: int = 0) -> dict:
    key = jax.random.key(seed)
    ks = jax.random.split(key, 28)
    f32 = jnp.float32
    nrm = lambda k, shape, s: jax.random.normal(k, shape, f32) * s
    gain = lambda k, shape: 1.0 + 0.01 * jax.random.normal(k, shape, f32)
    L_, G, P, C = DEPTH, N_SSM_GROUPS, SSM_STATE, SSM_GROUP_CH
    x = jax.random.normal(ks[0], (BATCH, SEQ, D_MODEL), f32)
    lam_im = math.pi * jnp.broadcast_to(jnp.arange(P, dtype=f32), (L_, G, P)) + 0.01 * jax.random.normal(ks[6], (L_, G, P), f32)
    log_step = jax.random.uniform(ks[7], (L_, G), f32, math.log(DT_MIN), math.log(DT_MAX))
    return {
        "x": x,
        "ln1_g": gain(ks[1], (L_, D_MODEL)),
        "w_in": nrm(ks[2], (L_, D_MODEL, IN_COLS), D_MODEL ** -0.5),
        "b_in": nrm(ks[3], (L_, IN_COLS), 0.02),
        "sinks": nrm(ks[4], (L_, N_Q_HEADS), 1.0),
        "lam_re": -0.5 + 0.01 * jax.random.normal(ks[5], (L_, G, P), f32),
        "lam_im": lam_im,
        "log_step": log_step,
        "ssm_b_re": nrm(ks[8], (L_, G, P, C), (2 * C) ** -0.5),
        "ssm_b_im": nrm(ks[9], (L_, G, P, C), (2 * C) ** -0.5),
        "ssm_c_re": nrm(ks[10], (L_, G, C, P), (2 * P) ** -0.5),
        "ssm_c_im": nrm(ks[11], (L_, G, C, P), (2 * P) ** -0.5),
        "ssm_d": nrm(ks[12], (L_, G, C), 1.0),
        "w_glu": nrm(ks[13], (L_, SSM_WIDTH, SSM_WIDTH), SSM_WIDTH ** -0.5),
        "b_glu": nrm(ks[14], (L_, SSM_WIDTH), 0.02),
        "g_attn": gain(ks[15], (L_, ATTN_WIDTH)),
        "g_ssm": gain(ks[16], (L_, SSM_WIDTH)),
        "w_out": nrm(ks[17], (L_, MIX_WIDTH, D_MODEL), MIX_WIDTH ** -0.5),
        "ln2_g": gain(ks[18], (L_, D_MODEL)),
        "w_up": nrm(ks[19], (L_, D_MODEL, 2 * D_FF), D_MODEL ** -0.5),
        "conv_w": nrm(ks[20], (L_, CONV_WIDTH, 2 * D_FF), CONV_WIDTH ** -0.5),
        "conv_b": nrm(ks[21], (L_, 2 * D_FF), 0.02),
        "w_down": nrm(ks[22], (L_, D_FF, D_MODEL), D_FF ** -0.5),
        "lnf_g": gain(ks[23], (D_MODEL,)),
    }


def reference(x, ln1_g, w_in, b_in, sinks, lam_re, lam_im, log_step, ssm_b_re, ssm_b_im,
              ssm_c_re, ssm_c_im, ssm_d, w_glu, b_glu, g_attn, g_ssm, w_out, ln2_g,
              w_up, conv_w, conv_b, w_down, lnf_g):
    b, l = x.shape[0], x.shape[1]
    pos = jnp.arange(l, dtype=jnp.float32)
    h = x
    for i in range(DEPTH):
        hn = rms_norm(h, ln1_g[i])
        proj = hn @ w_in[i] + b_in[i]
        q = proj[..., :Q_COLS].reshape(b, l, N_Q_HEADS, HEAD_DIM)
        k = proj[..., Q_COLS:Q_COLS + KV_COLS].reshape(b, l, N_KV_HEADS, HEAD_DIM)
        v = proj[..., Q_COLS + KV_COLS:Q_COLS + 2 * KV_COLS].reshape(b, l, N_KV_HEADS, HEAD_DIM)
        u = proj[..., Q_COLS + 2 * KV_COLS:]
        q = partial_rotary(q, pos)
        k = partial_rotary(k, pos)
        attn = sliding_window_gqa(q, k, v, sinks[i])
        y = jax.nn.gelu(s5_ssm(u, lam_re[i], lam_im[i], log_step[i], ssm_b_re[i], ssm_b_im[i],
                               ssm_c_re[i], ssm_c_im[i], ssm_d[i]))
        ssm = y * jax.nn.sigmoid(y @ w_glu[i] + b_glu[i])
        mix = jnp.concatenate([rms_norm(attn, g_attn[i]), rms_norm(ssm, g_ssm[i])], axis=-1)
        h = h + mix @ w_out[i]
        h = h + conv_ffn(rms_norm(h, ln2_g[i]), w_up[i], conv_w[i], conv_b[i], w_down[i])
    return rms_norm(h, lnf_g)
```

```python
import functools
import math

import jax
import jax.numpy as jnp
from jax import lax
from jax.experimental import pallas as pl
from jax.experimental.pallas import tpu as pltpu

D_MODEL = 1024
HEAD_DIM = 64
N_Q_HEADS = 8
N_KV_HEADS = 2
KV_GROUP = N_Q_HEADS // N_KV_HEADS
ATTN_WIDTH = N_Q_HEADS * HEAD_DIM
KV_COLS = N_KV_HEADS * HEAD_DIM
SSM_WIDTH = 512
WINDOW = 128
BLOCK = 128
ROPE_THETA = 500000.0
ROT_DIM = HEAD_DIM // 4
ROT_HALF = ROT_DIM // 2
GROUP_CH = 16
N_GROUPS = SSM_WIDTH // GROUP_CH
N_PAIRS = N_GROUPS // 2
SSM_STATE = 64
D_FF = 2816
CONV_WIDTH = 3
NORM_EPS = 1e-5
MASK_VALUE = -1e30

LANES = 128
SUBLANES = 8
CHUNK = 16
CHUNK_W = CHUNK * GROUP_CH
GROUPS_PER_SLAB = LANES // GROUP_CH
N_SLABS = SSM_WIDTH // LANES
FF_CHUNK = 256
N_FF_CHUNKS = D_FF // FF_CHUNK
VMEM_LIMIT = 56 * 1024 * 1024

TOK_TILE = 512
FFN_TILE = 256

_F32 = jnp.float32
_BF16 = jnp.bfloat16


def _rms(x, g):
    return x * lax.rsqrt(jnp.mean(x * x, axis=-1, keepdims=True) + NORM_EPS) * g


def _dot(a, b):
    return jnp.dot(a, b, preferred_element_type=_F32)


def _proj_kernel(x_ref, g_ref, w_ref, b_ref, freq_ref, q_ref, k_ref, v_ref, u_ref, us_ref, *, tiles_per_seq):
    tm = x_ref.shape[0]
    hn = _rms(x_ref[...], g_ref[...]).astype(_BF16)
    proj = _dot(hn, w_ref[...]) + b_ref[...]

    t0 = (pl.program_id(0) % tiles_per_seq) * tm
    pos = (t0 + lax.broadcasted_iota(jnp.int32, (tm, LANES), 0)).astype(_F32)
    ang = pos * freq_ref[...]
    cos, sin = jnp.cos(ang), jnp.sin(ang)
    hd = lax.broadcasted_iota(jnp.int32, (tm, LANES), 1) % HEAD_DIM
    sin_lo = jnp.where(hd < ROT_HALF, -sin, 0.0)
    sin_hi = jnp.where((hd >= ROT_HALF) & (hd < ROT_DIM), sin, 0.0)

    def rot(blk):
        return blk * cos + pltpu.roll(blk, LANES - ROT_HALF, 1) * sin_lo + pltpu.roll(blk, ROT_HALF, 1) * sin_hi

    scale = HEAD_DIM ** -0.5
    for c in range(ATTN_WIDTH // LANES):
        q_ref[:, c * LANES:(c + 1) * LANES] = (rot(proj[:, c * LANES:(c + 1) * LANES]) * scale).astype(_BF16)
    k0 = ATTN_WIDTH
    for c in range(2 * KV_COLS // LANES):
        k_ref[:, c * LANES:(c + 1) * LANES] = rot(proj[:, k0 + c * LANES:k0 + (c + 1) * LANES]).astype(_BF16)
    v0 = k0 + 2 * KV_COLS
    v_ref[...] = proj[:, v0:v0 + 2 * KV_COLS].astype(_BF16)

    u0 = v0 + 2 * KV_COLS
    for j in range(N_SLABS):
        us_ref[j] = proj[:, u0 + j * LANES:u0 + (j + 1) * LANES]
    n_chunks = tm // CHUNK
    lane_blk = lax.broadcasted_iota(jnp.int32, (n_chunks, LANES), 1) // GROUP_CH
    for j in range(N_SLABS):
        for h in range(CHUNK // GROUPS_PER_SLAB):
            src = [us_ref[j, pl.ds(h * GROUPS_PER_SLAB + tt, n_chunks, stride=CHUNK), :]
                   for tt in range(GROUPS_PER_SLAB)]
            for gm in range(GROUPS_PER_SLAB):
                acc = jnp.zeros((n_chunks, LANES), _F32)
                for tt in range(GROUPS_PER_SLAB):
                    shift = (GROUP_CH * (tt - gm)) % LANES
                    moved = src[tt] if shift == 0 else pltpu.roll(src[tt], shift, 1)
                    acc = jnp.where(lane_blk == tt, moved, acc)
                u_ref[j * GROUPS_PER_SLAB + gm, :, h * LANES:(h + 1) * LANES] = acc.astype(_BF16)


def _proj_call(x2, ln1_g, w_ext, b_ext, freq, seq):
    n = x2.shape[0]
    tm = TOK_TILE
    nc = w_ext.shape[1]
    const = lambda i: (0, 0)
    return pl.pallas_call(
        functools.partial(_proj_kernel, tiles_per_seq=seq // tm),
        grid=(n // tm,),
        in_specs=[
            pl.BlockSpec((tm, D_MODEL), lambda i: (i, 0)),
            pl.BlockSpec((1, D_MODEL), const),
            pl.BlockSpec((D_MODEL, nc), const),
            pl.BlockSpec((1, nc), const),
            pl.BlockSpec((1, LANES), const),
        ],
        out_specs=[
            pl.BlockSpec((tm, ATTN_WIDTH), lambda i: (i, 0)),
            pl.BlockSpec((tm, 2 * KV_COLS), lambda i: (i, 0)),
            pl.BlockSpec((tm, 2 * KV_COLS), lambda i: (i, 0)),
            pl.BlockSpec((N_GROUPS, tm // CHUNK, CHUNK_W), lambda i: (0, i, 0)),
        ],
        out_shape=[
            jax.ShapeDtypeStruct((n, ATTN_WIDTH), _BF16),
            jax.ShapeDtypeStruct((n, 2 * KV_COLS), _BF16),
            jax.ShapeDtypeStruct((n, 2 * KV_COLS), _BF16),
            jax.ShapeDtypeStruct((N_GROUPS, n // CHUNK, CHUNK_W), _BF16),
        ],
        scratch_shapes=[pltpu.VMEM((N_SLABS, tm, LANES), _F32)],
        compiler_params=pltpu.CompilerParams(dimension_semantics=("arbitrary",), vmem_limit_bytes=VMEM_LIMIT),
        name="proj",
    )(x2, ln1_g, w_ext, b_ext, freq)


def _attn_kernel(sink_ref, q_ref, kc_ref, kp_ref, vc_ref, vp_ref, g_ref, o_ref):
    tq = q_ref.shape[0]
    first_tile = pl.program_id(1) == 0
    lane = lax.broadcasted_iota(jnp.int32, (BLOCK, LANES), 1)
    lo = lane < HEAD_DIM
    qi = lax.broadcasted_iota(jnp.int32, (KV_GROUP * BLOCK, 2 * BLOCK), 0) % BLOCK
    kj = lax.broadcasted_iota(jnp.int32, (KV_GROUP * BLOCK, 2 * BLOCK), 1)
    band = (kj > qi) & (kj <= qi + WINDOW)
    zero = jnp.zeros((BLOCK, LANES), _BF16)
    for j in range(tq // BLOCK):
        rows = slice(j * BLOCK, (j + 1) * BLOCK)
        if j == 0:
            valid = band & ((kj >= BLOCK) | jnp.logical_not(first_tile))
        else:
            valid = band
        cols = []
        for hk in range(N_KV_HEADS):
            kv_cols = slice(hk * LANES, (hk + 1) * LANES)
            if j == 0:
                k_prev, v_prev = kp_ref[:, kv_cols], vp_ref[:, kv_cols]
            else:
                prev_rows = slice((j - 1) * BLOCK, j * BLOCK)
                k_prev, v_prev = kc_ref[prev_rows, kv_cols], vc_ref[prev_rows, kv_cols]
            kcat = jnp.concatenate([k_prev, kc_ref[rows, kv_cols]], axis=0)
            vcat = jnp.concatenate([v_prev, vc_ref[rows, kv_cols]], axis=0)
            qa = q_ref[rows, (2 * hk) * LANES:(2 * hk + 1) * LANES]
            qb = q_ref[rows, (2 * hk + 1) * LANES:(2 * hk + 2) * LANES]
            qs = jnp.concatenate([jnp.where(lo, qa, zero), jnp.where(lo, zero, qa),
                                  jnp.where(lo, qb, zero), jnp.where(lo, zero, qb)], axis=0)
            s = lax.dot_general(qs, kcat, (((1,), (1,)), ((), ())), preferred_element_type=_F32)
            s = jnp.where(valid, s, MASK_VALUE)
            sink = jnp.concatenate([jnp.full((BLOCK, 1), sink_ref[KV_GROUP * hk + a], _F32)
                                    for a in range(KV_GROUP)], axis=0)
            m = jnp.maximum(jnp.max(s, axis=-1, keepdims=True), sink)
            p = jnp.exp(s - m)
            denom = jnp.sum(p, axis=-1, keepdims=True) + jnp.exp(sink - m)
            o = _dot(p.astype(_BF16), vcat) / denom
            cols.append(jnp.where(lo, o[0:BLOCK], o[BLOCK:2 * BLOCK]))
            cols.append(jnp.where(lo, o[2 * BLOCK:3 * BLOCK], o[3 * BLOCK:4 * BLOCK]))
        attn = jnp.concatenate(cols, axis=1)
        o_ref[rows, :] = _rms(attn, g_ref[...]).astype(_BF16)


def _attn_call(sinks, q, kd, vd, g_attn, batch, seq):
    tq = TOK_TILE
    nt = seq // tq
    bpt = tq // BLOCK
    cur = lambda b, i, *_: (b * nt + i, 0)
    prev = lambda b, i, *_: (jnp.maximum((b * nt + i) * bpt - 1, 0), 0)
    grid_spec = pltpu.PrefetchScalarGridSpec(
        num_scalar_prefetch=1,
        grid=(batch, nt),
        in_specs=[
            pl.BlockSpec((tq, ATTN_WIDTH), cur),
            pl.BlockSpec((tq, 2 * KV_COLS), cur),
            pl.BlockSpec((BLOCK, 2 * KV_COLS), prev),
            pl.BlockSpec((tq, 2 * KV_COLS), cur),
            pl.BlockSpec((BLOCK, 2 * KV_COLS), prev),
            pl.BlockSpec((1, ATTN_WIDTH), lambda b, i, *_: (0, 0)),
        ],
        out_specs=pl.BlockSpec((tq, ATTN_WIDTH), cur),
    )
    return pl.pallas_call(
        _attn_kernel,
        grid_spec=grid_spec,
        out_shape=jax.ShapeDtypeStruct(q.shape, _BF16),
        compiler_params=pltpu.CompilerParams(dimension_semantics=("arbitrary", "arbitrary"),
                                             vmem_limit_bytes=VMEM_LIMIT),
        name="attn",
    )(sinks, q, kd, kd, vd, vd, g_attn)


def _ssm_kernel(u_ref, m_ref, e_ref, f_ref, cr_ref, ci_ref, pr_ref, pi_ref, y_ref, sr_ref, si_ref, xp_ref,
                *, chunks_per_seq):
    n_chunks = u_ref.shape[1]
    u0, u1 = u_ref[0], u_ref[1]
    s = _dot(u0, e_ref[0, 0:CHUNK_W, :]) + _dot(u1, e_ref[0, CHUNK_W:2 * CHUNK_W, :])
    sr_ref[...] = s[:, 0:LANES]
    si_ref[...] = s[:, LANES:2 * LANES]

    rows_per_seq = chunks_per_seq // SUBLANES
    sub = lax.broadcasted_iota(jnp.int32, (SUBLANES, LANES), 0)

    def step(r, carry):
        cre, cim = carry
        fresh = (r % rows_per_seq) == 0
        cre = jnp.where(fresh, 0.0, cre)
        cim = jnp.where(fresh, 0.0, cim)
        base = pl.multiple_of(r * SUBLANES, SUBLANES)
        vr, vi = sr_ref[pl.ds(base, SUBLANES), :], si_ref[pl.ds(base, SUBLANES), :]
        for lvl, d in enumerate((1, 2, 4)):
            ar, ai = cr_ref[0, lvl], ci_ref[0, lvl]
            wr, wi = pltpu.roll(vr, d, 0), pltpu.roll(vi, d, 0)
            vr, vi = vr + ar * wr - ai * wi, vi + ar * wi + ai * wr
        pr, pi = pr_ref[0], pi_ref[0]
        xr = vr + pr * cre - pi * cim
        xi = vi + pr * cim + pi * cre
        prev_r = jnp.where(sub == 0, cre, pltpu.roll(xr, 1, 0))
        prev_i = jnp.where(sub == 0, cim, pltpu.roll(xi, 1, 0))
        xp_ref[pl.ds(base, SUBLANES), 0:LANES] = prev_r
        xp_ref[pl.ds(base, SUBLANES), LANES:2 * LANES] = prev_i
        last_r = jnp.broadcast_to(xr[SUBLANES - 1:SUBLANES, :], (SUBLANES, LANES))
        last_i = jnp.broadcast_to(xi[SUBLANES - 1:SUBLANES, :], (SUBLANES, LANES))
        return last_r, last_i

    zeros = jnp.zeros((SUBLANES, LANES), _F32)
    lax.fori_loop(0, n_chunks // SUBLANES, step, (zeros, zeros))

    xp = xp_ref[...].astype(_BF16)
    for a, ua in enumerate((u0, u1)):
        y = _dot(ua, m_ref[a]) + _dot(xp, f_ref[0, :, a * CHUNK_W:(a + 1) * CHUNK_W])
        y_ref[a] = jax.nn.gelu(y).astype(_BF16)


def _ssm_call(u_g, m_mat, e_mat, f_mat, c_re, c_im, p_re, p_im, seq):
    n_chunks = u_g.shape[1]
    pair3 = lambda i: (i, 0, 0)
    pair4 = lambda i: (i, 0, 0, 0)
    return pl.pallas_call(
        functools.partial(_ssm_kernel, chunks_per_seq=seq // CHUNK),
        grid=(N_PAIRS,),
        in_specs=[
            pl.BlockSpec((2, n_chunks, CHUNK_W), pair3),
            pl.BlockSpec((2, CHUNK_W, CHUNK_W), pair3),
            pl.BlockSpec((1, 2 * CHUNK_W, 2 * LANES), pair3),
            pl.BlockSpec((1, 2 * LANES, 2 * CHUNK_W), pair3),
            pl.BlockSpec((1, 3, SUBLANES, LANES), pair4),
            pl.BlockSpec((1, 3, SUBLANES, LANES), pair4),
            pl.BlockSpec((1, SUBLANES, LANES), pair3),
            pl.BlockSpec((1, SUBLANES, LANES), pair3),
        ],
        out_specs=pl.BlockSpec((2, n_chunks, CHUNK_W), pair3),
        out_shape=jax.ShapeDtypeStruct(u_g.shape, _BF16),
        scratch_shapes=[
            pltpu.VMEM((n_chunks, LANES), _F32),
            pltpu.VMEM((n_chunks, LANES), _F32),
            pltpu.VMEM((n_chunks, 2 * LANES), _F32),
        ],
        compiler_params=pltpu.CompilerParams(dimension_semantics=("arbitrary",), vmem_limit_bytes=VMEM_LIMIT),
        name="ssm",
    )(u_g, m_mat, e_mat, f_mat, c_re, c_im, p_re, p_im)


def _glu_kernel(y_ref, w_ref, b_ref, g_ref, o_ref, ys_ref):
    n_chunks = y_ref.shape[1]
    lane_blk = lax.broadcasted_iota(jnp.int32, (n_chunks, LANES), 1) // GROUP_CH
    for j in range(N_SLABS):
        for h in range(CHUNK // GROUPS_PER_SLAB):
            src = [y_ref[j * GROUPS_PER_SLAB + gm, :, h * LANES:(h + 1) * LANES].astype(_F32)
                   for gm in range(GROUPS_PER_SLAB)]
            for tt in range(GROUPS_PER_SLAB):
                acc = jnp.zeros((n_chunks, LANES), _F32)
                for gm in range(GROUPS_PER_SLAB):
                    shift = (GROUP_CH * (gm - tt)) % LANES
                    moved = src[gm] if shift == 0 else pltpu.roll(src[gm], shift, 1)
                    acc = jnp.where(lane_blk == gm, moved, acc)
                ys_ref[j, pl.ds(h * GROUPS_PER_SLAB + tt, n_chunks, stride=CHUNK), :] = acc
    y = jnp.concatenate([ys_ref[j] for j in range(N_SLABS)], axis=1)
    z = _dot(y.astype(_BF16), w_ref[...]) + b_ref[...]
    o_ref[...] = _rms(y * jax.nn.sigmoid(z), g_ref[...]).astype(_BF16)


def _glu_call(y_g, w_glu, b_glu, g_ssm):
    n = y_g.shape[1] * CHUNK
    tm = TOK_TILE
    const = lambda i: (0, 0)
    return pl.pallas_call(
        _glu_kernel,
        grid=(n // tm,),
        in_specs=[
            pl.BlockSpec((N_GROUPS, tm // CHUNK, CHUNK_W), lambda i: (0, i, 0)),
            pl.BlockSpec((SSM_WIDTH, SSM_WIDTH), const),
            pl.BlockSpec((1, SSM_WIDTH), const),
            pl.BlockSpec((1, SSM_WIDTH), const),
        ],
        out_specs=pl.BlockSpec((tm, SSM_WIDTH), lambda i: (i, 0)),
        out_shape=jax.ShapeDtypeStruct((n, SSM_WIDTH), _BF16),
        scratch_shapes=[pltpu.VMEM((N_SLABS, tm, LANES), _F32)],
        compiler_params=pltpu.CompilerParams(dimension_semantics=("arbitrary",), vmem_limit_bytes=VMEM_LIMIT),
        name="glu",
    )(y_g, w_glu, b_glu, g_ssm)


def _ffn_kernel(x_ref, ma_ref, ms_ref, wo_ref, g2_ref, wu_ref, cw_ref, cb_ref, wd_ref, gf_ref, o_ref,
                carry_ref, buf_ref, acc_ref, hn_ref, *, tiles_per_seq):
    tm = x_ref.shape[0]
    first = (pl.program_id(0) % tiles_per_seq) == 0
    h1 = x_ref[...] + _dot(ma_ref[...], wo_ref[0:ATTN_WIDTH, :]) + _dot(ms_ref[...], wo_ref[ATTN_WIDTH:, :])
    hn_ref[...] = _rms(h1, g2_ref[...]).astype(_BF16)
    acc_ref[...] = h1

    @pl.when(first)
    def _():
        carry_ref[...] = jnp.zeros(carry_ref.shape, _F32)

    def conv_part(c):
        up = _dot(hn_ref[...], wu_ref[c])
        buf_ref[0:SUBLANES, :] = carry_ref[c]
        buf_ref[SUBLANES:, :] = up
        carry_ref[c] = up[tm - SUBLANES:, :]
        w = cw_ref[c]
        return (w[2:3, :] * up + w[1:2, :] * buf_ref[pl.ds(SUBLANES - 1, tm), :]
                + w[0:1, :] * buf_ref[pl.ds(SUBLANES - 2, tm), :] + cb_ref[c])

    def chunk(j, carry):
        gate = conv_part(j)
        val = conv_part(j + N_FF_CHUNKS)
        act = (jax.nn.silu(gate) * val).astype(_BF16)
        acc_ref[...] += _dot(act, wd_ref[j])
        return carry

    lax.fori_loop(0, N_FF_CHUNKS, chunk, 0)
    o_ref[...] = _rms(acc_ref[...], gf_ref[...])


def _ffn_call(x2, mix_a, mix_s, w_out, ln2_g, w_up_c, conv_w_c, conv_b_c, w_down_c, lnf_g, seq):
    n = x2.shape[0]
    tm = FFN_TILE
    const2 = lambda i: (0, 0)
    const3 = lambda i: (0, 0, 0)
    once = pl.Buffered(1)
    return pl.pallas_call(
        functools.partial(_ffn_kernel, tiles_per_seq=seq // tm),
        grid=(n // tm,),
        in_specs=[
            pl.BlockSpec((tm, D_MODEL), lambda i: (i, 0)),
            pl.BlockSpec((tm, ATTN_WIDTH), lambda i: (i, 0)),
            pl.BlockSpec((tm, SSM_WIDTH), lambda i: (i, 0)),
            pl.BlockSpec((D_MODEL, D_MODEL), const2, pipeline_mode=once),
            pl.BlockSpec((1, D_MODEL), const2),
            pl.BlockSpec((2 * N_FF_CHUNKS, D_MODEL, FF_CHUNK), const3, pipeline_mode=once),
            pl.BlockSpec((2 * N_FF_CHUNKS, CONV_WIDTH, FF_CHUNK), const3),
            pl.BlockSpec((2 * N_FF_CHUNKS, 1, FF_CHUNK), const3),
            pl.BlockSpec((N_FF_CHUNKS, FF_CHUNK, D_MODEL), const3, pipeline_mode=once),
            pl.BlockSpec((1, D_MODEL), const2),
        ],
        out_specs=pl.BlockSpec((tm, D_MODEL), lambda i: (i, 0)),
        out_shape=jax.ShapeDtypeStruct((n, D_MODEL), _F32),
        scratch_shapes=[
            pltpu.VMEM((2 * N_FF_CHUNKS, SUBLANES, FF_CHUNK), _F32),
            pltpu.VMEM((tm + SUBLANES, FF_CHUNK), _F32),
            pltpu.VMEM((tm, D_MODEL), _F32),
            pltpu.VMEM((tm, D_MODEL), _BF16),
        ],
        compiler_params=pltpu.CompilerParams(dimension_semantics=("arbitrary",), vmem_limit_bytes=VMEM_LIMIT),
        name="ffn",
    )(x2, mix_a, mix_s, w_out, ln2_g, w_up_c, conv_w_c, conv_b_c, w_down_c, lnf_g)


def _ssm_operators(lam_re, lam_im, log_step, b_re, b_im, c_re, c_im, d_skip):
    g, p_dim = lam_re.shape
    lam = lax.complex(lam_re, lam_im)
    dt = jnp.exp(log_step)[:, None]
    a = jnp.exp(lam * dt)
    bbar = ((a - 1.0) / lam)[..., None] * lax.complex(b_re, b_im)
    c_c = lax.complex(c_re, c_im)
    pows = [jnp.ones_like(a)]
    for _ in range(CHUNK):
        pows.append(pows[-1] * a)
    pw = jnp.stack(pows, axis=0)

    kern = jnp.einsum('gcp,tgp,gpd->gtcd', c_c, pw[:CHUNK], bbar).real
    lag = jnp.arange(CHUNK)[None, :] - jnp.arange(CHUNK)[:, None]
    toe = jnp.where((lag >= 0)[None, :, :, None, None], kern[:, jnp.maximum(lag, 0)], 0.0)
    eye_t = jnp.eye(CHUNK, dtype=_F32)[None, :, :, None, None]
    eye_c = jnp.eye(GROUP_CH, dtype=_F32)[None, None, None, :, :]
    toe = toe + eye_t * eye_c * d_skip[:, None, None, :, None]
    m_mat = toe.transpose(0, 1, 4, 2, 3).reshape(g, CHUNK_W, CHUNK_W)

    e_c = pw[:CHUNK][::-1].transpose(1, 0, 2)[:, :, None, :] * bbar.transpose(0, 2, 1)[:, None, :, :]
    e_c = e_c.reshape(g, CHUNK_W, p_dim)
    f_c = (c_c[:, None, :, :] * pw[1:].transpose(1, 0, 2)[:, :, None, :]).reshape(g, CHUNK_W, p_dim)
    f_c = f_c.transpose(0, 2, 1)

    z_e = jnp.zeros((g // 2, CHUNK_W, p_dim), _F32)
    e0, e1 = e_c[0::2], e_c[1::2]
    e_pair = jnp.concatenate([
        jnp.concatenate([e0.real, z_e, e0.imag, z_e], axis=2),
        jnp.concatenate([z_e, e1.real, z_e, e1.imag], axis=2)], axis=1)
    z_f = jnp.zeros((g // 2, p_dim, CHUNK_W), _F32)
    f0, f1 = f_c[0::2], f_c[1::2]
    f_pair = jnp.concatenate([
        jnp.concatenate([f0.real, z_f], axis=2), jnp.concatenate([z_f, f1.real], axis=2),
        jnp.concatenate([-f0.imag, z_f], axis=2), jnp.concatenate([z_f, -f1.imag], axis=2)], axis=1)

    big = pw[CHUNK]
    lanes = lambda v: jnp.concatenate([v[0::2], v[1::2]], axis=-1)
    big_pows = [big]
    for _ in range(SUBLANES - 1):
        big_pows.append(big_pows[-1] * big)
    sub = jnp.arange(SUBLANES)[None, :, None]
    lvl = jnp.stack([jnp.where(sub >= d, lanes(big_pows[d - 1])[:, None, :], 0.0) for d in (1, 2, 4)], axis=1)
    carry = jnp.stack([lanes(bp) for bp in big_pows], axis=1)
    return (m_mat.astype(_BF16), e_pair.astype(_BF16), f_pair.astype(_BF16),
            lvl.real, lvl.imag, carry.real, carry.imag)


def kernel(x, ln1_g, w_in, b_in, sinks, lam_re, lam_im, log_step, ssm_b_re, ssm_b_im, ssm_c_re, ssm_c_im, ssm_d,
           w_glu, b_glu, g_attn, g_ssm, w_out, ln2_g, w_up, conv_w, conv_b, w_down, lnf_g):
    batch, seq, _ = x.shape
    assert ln1_g.shape[0] == 1, "single-layer trunk"
    assert seq % TOK_TILE == 0 and seq % (CHUNK * SUBLANES) == 0
    n = batch * seq
    x2 = x.reshape(n, D_MODEL)

    wq, wk, wv, wu = (w_in[0][:, :ATTN_WIDTH], w_in[0][:, ATTN_WIDTH:ATTN_WIDTH + KV_COLS],
                      w_in[0][:, ATTN_WIDTH + KV_COLS:ATTN_WIDTH + 2 * KV_COLS], w_in[0][:, ATTN_WIDTH + 2 * KV_COLS:])
    bq, bk, bv, bu = (b_in[0][:ATTN_WIDTH], b_in[0][ATTN_WIDTH:ATTN_WIDTH + KV_COLS],
                      b_in[0][ATTN_WIDTH + KV_COLS:ATTN_WIDTH + 2 * KV_COLS], b_in[0][ATTN_WIDTH + 2 * KV_COLS:])
    dup = lambda t: jnp.concatenate([t[..., :HEAD_DIM], t[..., :HEAD_DIM], t[..., HEAD_DIM:], t[..., HEAD_DIM:]], axis=-1)
    w_ext = jnp.concatenate([wq, dup(wk), dup(wv), wu], axis=1).astype(_BF16)
    b_ext = jnp.concatenate([bq, dup(bk), dup(bv), bu])[None, :]
    inv_freq = ROPE_THETA ** (-jnp.arange(ROT_HALF, dtype=_F32) * 2.0 / ROT_DIM)
    head_lane = jnp.arange(LANES) % HEAD_DIM
    freq = jnp.where(head_lane < ROT_DIM, inv_freq[head_lane % ROT_HALF], 0.0)[None, :]

    q, kd, vd, u_g = _proj_call(x2, ln1_g, w_ext, b_ext, freq, seq)
    mix_a = _attn_call(sinks[0], q, kd, vd, g_attn, batch, seq)

    ops = _ssm_operators(lam_re[0], lam_im[0], log_step[0], ssm_b_re[0], ssm_b_im[0], ssm_c_re[0], ssm_c_im[0], ssm_d[0])
    y_g = _ssm_call(u_g, *ops, seq)
    mix_s = _glu_call(y_g, w_glu[0].astype(_BF16), b_glu, g_ssm)

    w_up_c = w_up[0].reshape(D_MODEL, 2 * N_FF_CHUNKS, FF_CHUNK).transpose(1, 0, 2).astype(_BF16)
    conv_w_c = conv_w[0].reshape(CONV_WIDTH, 2 * N_FF_CHUNKS, FF_CHUNK).transpose(1, 0, 2)
    conv_b_c = conv_b[0].reshape(2 * N_FF_CHUNKS, 1, FF_CHUNK)
    w_down_c = w_down[0].reshape(N_FF_CHUNKS, FF_CHUNK, D_MODEL).astype(_BF16)
    out = _ffn_call(x2, mix_a, mix_s, w_out[0].astype(_BF16), ln2_g, w_up_c, conv_w_c, conv_b_c, w_down_c,
                    lnf_g[None, :], seq)
    return out.reshape(batch, seq, D_MODEL)
```

```python
import functools
import math

import jax
import jax.numpy as jnp
from jax import lax
from jax.experimental import pallas as pl
from jax.experimental.pallas import tpu as pltpu

D_MODEL = 1024
HEAD_DIM = 64
N_Q_HEADS = 8
N_KV_HEADS = 2
KV_GROUP = N_Q_HEADS // N_KV_HEADS
ATTN_WIDTH = N_Q_HEADS * HEAD_DIM
KV_COLS = N_KV_HEADS * HEAD_DIM
SSM_WIDTH = 512
WINDOW = 128
BLOCK = 128
ROPE_THETA = 500000.0
ROT_DIM = HEAD_DIM // 4
ROT_HALF = ROT_DIM // 2
GROUP_CH = 16
N_GROUPS = SSM_WIDTH // GROUP_CH
N_PAIRS = N_GROUPS // 2
SSM_STATE = 64
D_FF = 2816
CONV_WIDTH = 3
NORM_EPS = 1e-5
MASK_VALUE = -1e30

LANES = 128
SUBLANES = 8
CHUNK = 16
CHUNK_W = CHUNK * GROUP_CH
GROUPS_PER_SLAB = LANES // GROUP_CH
N_SLABS = SSM_WIDTH // LANES
FF_CHUNK = 256
N_FF_CHUNKS = D_FF // FF_CHUNK
VMEM_LIMIT = 56 * 1024 * 1024

TOK_TILE = 512
FFN_TILE = 256

_F32 = jnp.float32
_BF16 = jnp.bfloat16


def _rms(x, g):
    return x * lax.rsqrt(jnp.mean(x * x, axis=-1, keepdims=True) + NORM_EPS) * g


def _dot(a, b):
    return jnp.dot(a, b, preferred_element_type=_F32)


def _proj_kernel(x_ref, g_ref, w_ref, b_ref, freq_ref, q_ref, k_ref, v_ref, u_ref, us_ref, *, tiles_per_seq):
    tm = x_ref.shape[0]
    hn = _rms(x_ref[...], g_ref[...]).astype(_BF16)
    proj = _dot(hn, w_ref[...]) + b_ref[...]

    t0 = (pl.program_id(0) % tiles_per_seq) * tm
    pos = (t0 + lax.broadcasted_iota(jnp.int32, (tm, LANES), 0)).astype(_F32)
    ang = pos * freq_ref[...]
    cos, sin = jnp.cos(ang), jnp.sin(ang)
    hd = lax.broadcasted_iota(jnp.int32, (tm, LANES), 1) % HEAD_DIM
    sin_lo = jnp.where(hd < ROT_HALF, -sin, 0.0)
    sin_hi = jnp.where((hd >= ROT_HALF) & (hd < ROT_DIM), sin, 0.0)

    def rot(blk):
        return blk * cos + pltpu.roll(blk, LANES - ROT_HALF, 1) * sin_lo + pltpu.roll(blk, ROT_HALF, 1) * sin_hi

    scale = HEAD_DIM ** -0.5
    for c in range(ATTN_WIDTH // LANES):
        q_ref[:, c * LANES:(c + 1) * LANES] = (rot(proj[:, c * LANES:(c + 1) * LANES]) * scale).astype(_BF16)
    k0 = ATTN_WIDTH
    for c in range(2 * KV_COLS // LANES):
        k_ref[:, c * LANES:(c + 1) * LANES] = rot(proj[:, k0 + c * LANES:k0 + (c + 1) * LANES]).astype(_BF16)
    v0 = k0 + 2 * KV_COLS
    v_ref[...] = proj[:, v0:v0 + 2 * KV_COLS].astype(_BF16)

    u0 = v0 + 2 * KV_COLS
    for j in range(N_SLABS):
        us_ref[j] = proj[:, u0 + j * LANES:u0 + (j + 1) * LANES]
    n_chunks = tm // CHUNK
    lane_blk = lax.broadcasted_iota(jnp.int32, (n_chunks, LANES), 1) // GROUP_CH
    for j in range(N_SLABS):
        for h in range(CHUNK // GROUPS_PER_SLAB):
            src = [us_ref[j, pl.ds(h * GROUPS_PER_SLAB + tt, n_chunks, stride=CHUNK), :]
                   for tt in range(GROUPS_PER_SLAB)]
            for gm in range(GROUPS_PER_SLAB):
                acc = jnp.zeros((n_chunks, LANES), _F32)
                for tt in range(GROUPS_PER_SLAB):
                    shift = (GROUP_CH * (tt - gm)) % LANES
                    moved = src[tt] if shift == 0 else pltpu.roll(src[tt], shift, 1)
                    acc = jnp.where(lane_blk == tt, moved, acc)
                u_ref[j * GROUPS_PER_SLAB + gm, :, h * LANES:(h + 1) * LANES] = acc.astype(_BF16)


def _proj_call(x2, ln1_g, w_ext, b_ext, freq, seq):
    n = x2.shape[0]
    tm = TOK_TILE
    nc = w_ext.shape[1]
    const = lambda i: (0, 0)
    return pl.pallas_call(
        functools.partial(_proj_kernel, tiles_per_seq=seq // tm),
        grid=(n // tm,),
        in_specs=[
            pl.BlockSpec((tm, D_MODEL), lambda i: (i, 0)),
            pl.BlockSpec((1, D_MODEL), const),
            pl.BlockSpec((D_MODEL, nc), const),
            pl.BlockSpec((1, nc), const),
            pl.BlockSpec((1, LANES), const),
        ],
        out_specs=[
            pl.BlockSpec((tm, ATTN_WIDTH), lambda i: (i, 0)),
            pl.BlockSpec((tm, 2 * KV_COLS), lambda i: (i, 0)),
            pl.BlockSpec((tm, 2 * KV_COLS), lambda i: (i, 0)),
            pl.BlockSpec((N_GROUPS, tm // CHUNK, CHUNK_W), lambda i: (0, i, 0)),
        ],
        out_shape=[
            jax.ShapeDtypeStruct((n, ATTN_WIDTH), _BF16),
            jax.ShapeDtypeStruct((n, 2 * KV_COLS), _BF16),
            jax.ShapeDtypeStruct((n, 2 * KV_COLS), _BF16),
            jax.ShapeDtypeStruct((N_GROUPS, n // CHUNK, CHUNK_W), _BF16),
        ],
        scratch_shapes=[pltpu.VMEM((N_SLABS, tm, LANES), _F32)],
        compiler_params=pltpu.CompilerParams(dimension_semantics=("arbitrary",), vmem_limit_bytes=VMEM_LIMIT),
        name="proj",
    )(x2, ln1_g, w_ext, b_ext, freq)


def _attn_kernel(sink_ref, q_ref, kc_ref, kp_ref, vc_ref, vp_ref, g_ref, o_ref):
    tq = q_ref.shape[0]
    first_tile = pl.program_id(1) == 0
    lane = lax.broadcasted_iota(jnp.int32, (BLOCK, LANES), 1)
    lo = lane < HEAD_DIM
    qi = lax.broadcasted_iota(jnp.int32, (KV_GROUP * BLOCK, 2 * BLOCK), 0) % BLOCK
    kj = lax.broadcasted_iota(jnp.int32, (KV_GROUP * BLOCK, 2 * BLOCK), 1)
    band = (kj > qi) & (kj <= qi + WINDOW)
    zero = jnp.zeros((BLOCK, LANES), _BF16)
    for j in range(tq // BLOCK):
        rows = slice(j * BLOCK, (j + 1) * BLOCK)
        if j == 0:
            valid = band & ((kj >= BLOCK) | jnp.logical_not(first_tile))
        else:
            valid = band
        cols = []
        for hk in range(N_KV_HEADS):
            kv_cols = slice(hk * LANES, (hk + 1) * LANES)
            if j == 0:
                k_prev, v_prev = kp_ref[:, kv_cols], vp_ref[:, kv_cols]
            else:
                prev_rows = slice((j - 1) * BLOCK, j * BLOCK)
                k_prev, v_prev = kc_ref[prev_rows, kv_cols], vc_ref[prev_rows, kv_cols]
            kcat = jnp.concatenate([k_prev, kc_ref[rows, kv_cols]], axis=0)
            vcat = jnp.concatenate([v_prev, vc_ref[rows, kv_cols]], axis=0)
            qa = q_ref[rows, (2 * hk) * LANES:(2 * hk + 1) * LANES]
            qb = q_ref[rows, (2 * hk + 1) * LANES:(2 * hk + 2) * LANES]
            qs = jnp.concatenate([jnp.where(lo, qa, zero), jnp.where(lo, zero, qa),
                                  jnp.where(lo, qb, zero), jnp.where(lo, zero, qb)], axis=0)
            s = lax.dot_general(qs, kcat, (((1,), (1,)), ((), ())), preferred_element_type=_F32)
            s = jnp.where(valid, s, MASK_VALUE)
            sink = jnp.concatenate([jnp.full((BLOCK, 1), sink_ref[KV_GROUP * hk + a], _F32)
                                    for a in range(KV_GROUP)], axis=0)
            m = jnp.maximum(jnp.max(s, axis=-1, keepdims=True), sink)
            p = jnp.exp(s - m)
            denom = jnp.sum(p, axis=-1, keepdims=True) + jnp.exp(sink - m)
            o = _dot(p.astype(_BF16), vcat) / denom
            cols.append(jnp.where(lo, o[0:BLOCK], o[BLOCK:2 * BLOCK]))
            cols.append(jnp.where(lo, o[2 * BLOCK:3 * BLOCK], o[3 * BLOCK:4 * BLOCK]))
        attn = jnp.concatenate(cols, axis=1)
        o_ref[rows, :] = _rms(attn, g_ref[...]).astype(_BF16)


def _attn_call(sinks, q, kd, vd, g_attn, batch, seq):
    tq = TOK_TILE
    nt = seq // tq
    bpt = tq // BLOCK
    cur = lambda b, i, *_: (b * nt + i, 0)
    prev = lambda b, i, *_: (jnp.maximum((b * nt + i) * bpt - 1, 0), 0)
    grid_spec = pltpu.PrefetchScalarGridSpec(
        num_scalar_prefetch=1,
        grid=(batch, nt),
        in_specs=[
            pl.BlockSpec((tq, ATTN_WIDTH), cur),
            pl.BlockSpec((tq, 2 * KV_COLS), cur),
            pl.BlockSpec((BLOCK, 2 * KV_COLS), prev),
            pl.BlockSpec((tq, 2 * KV_COLS), cur),
            pl.BlockSpec((BLOCK, 2 * KV_COLS), prev),
            pl.BlockSpec((1, ATTN_WIDTH), lambda b, i, *_: (0, 0)),
        ],
        out_specs=pl.BlockSpec((tq, ATTN_WIDTH), cur),
    )
    return pl.pallas_call(
        _attn_kernel,
        grid_spec=grid_spec,
        out_shape=jax.ShapeDtypeStruct(q.shape, _BF16),
        compiler_params=pltpu.CompilerParams(dimension_semantics=("arbitrary", "arbitrary"),
                                             vmem_limit_bytes=VMEM_LIMIT),
        name="attn",
    )(sinks, q, kd, kd, vd, vd, g_attn)


def _ssm_kernel(u_ref, m_ref, e_ref, f_ref, cr_ref, ci_ref, pr_ref, pi_ref, y_ref, sr_ref, si_ref, xp_ref,
                *, chunks_per_seq):
    n_chunks = u_ref.shape[1]
    u0, u1 = u_ref[0], u_ref[1]
    s = _dot(u0, e_ref[0, 0:CHUNK_W, :]) + _dot(u1, e_ref[0, CHUNK_W:2 * CHUNK_W, :])
    sr_ref[...] = s[:, 0:LANES]
    si_ref[...] = s[:, LANES:2 * LANES]

    rows_per_seq = chunks_per_seq // SUBLANES
    sub = lax.broadcasted_iota(jnp.int32, (SUBLANES, LANES), 0)

    def step(r, carry):
        cre, cim = carry
        fresh = (r % rows_per_seq) == 0
        cre = jnp.where(fresh, 0.0, cre)
        cim = jnp.where(fresh, 0.0, cim)
        base = pl.multiple_of(r * SUBLANES, SUBLANES)
        vr, vi = sr_ref[pl.ds(base, SUBLANES), :], si_ref[pl.ds(base, SUBLANES), :]
        for lvl, d in enumerate((1, 2, 4)):
            ar, ai = cr_ref[0, lvl], ci_ref[0, lvl]
            wr, wi = pltpu.roll(vr, d, 0), pltpu.roll(vi, d, 0)
            vr, vi = vr + ar * wr - ai * wi, vi + ar * wi + ai * wr
        pr, pi = pr_ref[0], pi_ref[0]
        xr = vr + pr * cre - pi * cim
        xi = vi + pr * cim + pi * cre
        prev_r = jnp.where(sub == 0, cre, pltpu.roll(xr, 1, 0))
        prev_i = jnp.where(sub == 0, cim, pltpu.roll(xi, 1, 0))
        xp_ref[pl.ds(base, SUBLANES), 0:LANES] = prev_r
        xp_ref[pl.ds(base, SUBLANES), LANES:2 * LANES] = prev_i
        last_r = jnp.broadcast_to(xr[SUBLANES - 1:SUBLANES, :], (SUBLANES, LANES))
        last_i = jnp.broadcast_to(xi[SUBLANES - 1:SUBLANES, :], (SUBLANES, LANES))
        return last_r, last_i

    zeros = jnp.zeros((SUBLANES, LANES), _F32)
    lax.fori_loop(0, n_chunks // SUBLANES, step, (zeros, zeros))

    xp = xp_ref[...].astype(_BF16)
    for a, ua in enumerate((u0, u1)):
        y = _dot(ua, m_ref[a]) + _dot(xp, f_ref[0, :, a * CHUNK_W:(a + 1) * CHUNK_W])
        y_ref[a] = jax.nn.gelu(y).astype(_BF16)


def _ssm_call(u_g, m_mat, e_mat, f_mat, c_re, c_im, p_re, p_im, seq):
    n_chunks = u_g.shape[1]
    pair3 = lambda i: (i, 0, 0)
    pair4 = lambda i: (i, 0, 0, 0)
    return pl.pallas_call(
        functools.partial(_ssm_kernel, chunks_per_seq=seq // CHUNK),
        grid=(N_PAIRS,),
        in_specs=[
            pl.BlockSpec((2, n_chunks, CHUNK_W), pair3),
            pl.BlockSpec((2, CHUNK_W, CHUNK_W), pair3),
            pl.BlockSpec((1, 2 * CHUNK_W, 2 * LANES), pair3),
            pl.BlockSpec((1, 2 * LANES, 2 * CHUNK_W), pair3),
            pl.BlockSpec((1, 3, SUBLANES, LANES), pair4),
            pl.BlockSpec((1, 3, SUBLANES, LANES), pair4),
            pl.BlockSpec((1, SUBLANES, LANES), pair3),
            pl.BlockSpec((1, SUBLANES, LANES), pair3),
        ],
        out_specs=pl.BlockSpec((2, n_chunks, CHUNK_W), pair3),
        out_shape=jax.ShapeDtypeStruct(u_g.shape, _BF16),
        scratch_shapes=[
            pltpu.VMEM((n_chunks, LANES), _F32),
            pltpu.VMEM((n_chunks, LANES), _F32),
            pltpu.VMEM((n_chunks, 2 * LANES), _F32),
        ],
        compiler_params=pltpu.CompilerParams(dimension_semantics=("arbitrary",), vmem_limit_bytes=VMEM_LIMIT),
        name="ssm",
    )(u_g, m_mat, e_mat, f_mat, c_re, c_im, p_re, p_im)


def _glu_kernel(y_ref, w_ref, b_ref, g_ref, o_ref, ys_ref):
    n_chunks = y_ref.shape[1]
    lane_blk = lax.broadcasted_iota(jnp.int32, (n_chunks, LANES), 1) // GROUP_CH
    for j in range(N_SLABS):
        for h in range(CHUNK // GROUPS_PER_SLAB):
            src = [y_ref[j * GROUPS_PER_SLAB + gm, :, h * LANES:(h + 1) * LANES].astype(_F32)
                   for gm in range(GROUPS_PER_SLAB)]
            for tt in range(GROUPS_PER_SLAB):
                acc = jnp.zeros((n_chunks, LANES), _F32)
                for gm in range(GROUPS_PER_SLAB):
                    shift = (GROUP_CH * (gm - tt)) % LANES
                    moved = src[gm] if shift == 0 else pltpu.roll(src[gm], shift, 1)
                    acc = jnp.where(lane_blk == gm, moved, acc)
                ys_ref[j, pl.ds(h * GROUPS_PER_SLAB + tt, n_chunks, stride=CHUNK), :] = acc
    y = jnp.concatenate([ys_ref[j] for j in range(N_SLABS)], axis=1)
    z = _dot(y.astype(_BF16), w_ref[...]) + b_ref[...]
    o_ref[...] = _rms(y * jax.nn.sigmoid(z), g_ref[...]).astype(_BF16)


def _glu_call(y_g, w_glu, b_glu, g_ssm):
    n = y_g.shape[1] * CHUNK
    tm = TOK_TILE
    const = lambda i: (0, 0)
    return pl.pallas_call(
        _glu_kernel,
        grid=(n // tm,),
        in_specs=[
            pl.BlockSpec((N_GROUPS, tm // CHUNK, CHUNK_W), lambda i: (0, i, 0)),
            pl.BlockSpec((SSM_WIDTH, SSM_WIDTH), const),
            pl.BlockSpec((1, SSM_WIDTH), const),
            pl.BlockSpec((1, SSM_WIDTH), const),
        ],
        out_specs=pl.BlockSpec((tm, SSM_WIDTH), lambda i: (i, 0)),
        out_shape=jax.ShapeDtypeStruct((n, SSM_WIDTH), _BF16),
        scratch_shapes=[pltpu.VMEM((N_SLABS, tm, LANES), _F32)],
        compiler_params=pltpu.CompilerParams(dimension_semantics=("arbitrary",), vmem_limit_bytes=VMEM_LIMIT),
        name="glu",
    )(y_g, w_glu, b_glu, g_ssm)


def _ffn_kernel(x_ref, ma_ref, ms_ref, wo_ref, g2_ref, wu_ref, cw_ref, cb_ref, wd_ref, gf_ref, o_ref,
                carry_ref, act_ref, hn_ref, *, tiles_per_seq):
    tm = x_ref.shape[0]
    first = (pl.program_id(0) % tiles_per_seq) == 0
    h1 = x_ref[...] + _dot(ma_ref[...], wo_ref[0:ATTN_WIDTH, :]) + _dot(ms_ref[...], wo_ref[ATTN_WIDTH:, :])
    hn_ref[...] = _rms(h1, g2_ref[...]).astype(_BF16)

    @pl.when(first)
    def _():
        carry_ref[...] = jnp.zeros(carry_ref.shape, _F32)

    def conv_part(c):
        cols = slice(c * FF_CHUNK, (c + 1) * FF_CHUNK)
        up = _dot(hn_ref[...], wu_ref[:, cols])
        ext = jnp.concatenate([carry_ref[:, cols], up], axis=0)
        carry_ref[:, cols] = up[tm - SUBLANES:, :]
        prev1 = pltpu.roll(ext, 1, 0)[SUBLANES:, :]
        prev2 = pltpu.roll(ext, 2, 0)[SUBLANES:, :]
        return (cw_ref[2:3, cols] * up + cw_ref[1:2, cols] * prev1 + cw_ref[0:1, cols] * prev2 + cb_ref[:, cols])

    for j in range(N_FF_CHUNKS):
        gate = conv_part(j)
        val = conv_part(j + N_FF_CHUNKS)
        act_ref[:, j * FF_CHUNK:(j + 1) * FF_CHUNK] = (jax.nn.silu(gate) * val).astype(_BF16)

    o_ref[...] = _rms(h1 + _dot(act_ref[...], wd_ref[...]), gf_ref[...])


def _ffn_call(x2, mix_a, mix_s, w_out, ln2_g, w_up, conv_w, conv_b, w_down, lnf_g, seq):
    n = x2.shape[0]
    tm = FFN_TILE
    const = lambda i: (0, 0)
    once = pl.Buffered(1)
    return pl.pallas_call(
        functools.partial(_ffn_kernel, tiles_per_seq=seq // tm),
        grid=(n // tm,),
        in_specs=[
            pl.BlockSpec((tm, D_MODEL), lambda i: (i, 0)),
            pl.BlockSpec((tm, ATTN_WIDTH), lambda i: (i, 0)),
            pl.BlockSpec((tm, SSM_WIDTH), lambda i: (i, 0)),
            pl.BlockSpec((D_MODEL, D_MODEL), const, pipeline_mode=once),
            pl.BlockSpec((1, D_MODEL), const),
            pl.BlockSpec((D_MODEL, 2 * D_FF), const, pipeline_mode=once),
            pl.BlockSpec((CONV_WIDTH, 2 * D_FF), const),
            pl.BlockSpec((1, 2 * D_FF), const),
            pl.BlockSpec((D_FF, D_MODEL), const, pipeline_mode=once),
            pl.BlockSpec((1, D_MODEL), const),
        ],
        out_specs=pl.BlockSpec((tm, D_MODEL), lambda i: (i, 0)),
        out_shape=jax.ShapeDtypeStruct((n, D_MODEL), _F32),
        scratch_shapes=[
            pltpu.VMEM((SUBLANES, 2 * D_FF), _F32),
            pltpu.VMEM((tm, D_FF), _BF16),
            pltpu.VMEM((tm, D_MODEL), _BF16),
        ],
        compiler_params=pltpu.CompilerParams(dimension_semantics=("arbitrary",), vmem_limit_bytes=VMEM_LIMIT),
        name="ffn",
    )(x2, mix_a, mix_s, w_out, ln2_g, w_up, conv_w, conv_b, w_down, lnf_g)


def _ssm_operators(lam_re, lam_im, log_step, b_re, b_im, c_re, c_im, d_skip):
    g, p_dim = lam_re.shape
    dt = jnp.exp(log_step)[:, None]
    zr, zi = lam_re * dt, lam_im * dt
    expo = jnp.concatenate([jnp.arange(CHUNK + 1), CHUNK * jnp.arange(2, SUBLANES + 1)]).astype(_F32)[:, None, None]
    mag = jnp.exp(expo * zr)
    pr, pi = mag * jnp.cos(expo * zi), mag * jnp.sin(expo * zi)
    den = lam_re * lam_re + lam_im * lam_im
    qr = ((pr[1] - 1.0) * lam_re + pi[1] * lam_im) / den
    qi = (pi[1] * lam_re - (pr[1] - 1.0) * lam_im) / den
    bbr = qr[..., None] * b_re - qi[..., None] * b_im
    bbi = qr[..., None] * b_im + qi[..., None] * b_re
    wr = pr[:CHUNK, :, :, None] * bbr - pi[:CHUNK, :, :, None] * bbi
    wi = pr[:CHUNK, :, :, None] * bbi + pi[:CHUNK, :, :, None] * bbr

    kern = jnp.einsum('gcp,tgpd->gdtc', c_re, wr) - jnp.einsum('gcp,tgpd->gdtc', c_im, wi)
    base = jnp.concatenate([kern, jnp.zeros((g, GROUP_CH, CHUNK + 1, GROUP_CH), _F32)], axis=2)
    tiled = jnp.broadcast_to(base[:, :, None], (g, GROUP_CH, CHUNK, 2 * CHUNK + 1, GROUP_CH))
    tiled = tiled.reshape(g, GROUP_CH, CHUNK * (2 * CHUNK + 1), GROUP_CH)[:, :, :2 * CHUNK * CHUNK]
    toe = tiled.reshape(g, GROUP_CH, CHUNK, 2 * CHUNK, GROUP_CH)[:, :, :, :CHUNK]
    skip = (jnp.eye(GROUP_CH, dtype=_F32)[None, :, None, None, :] * jnp.eye(CHUNK, dtype=_F32)[None, None, :, :, None]
            * d_skip[:, None, None, None, :])
    m_mat = (toe + skip).transpose(0, 2, 1, 3, 4).reshape(g, CHUNK_W, CHUNK_W)

    to_rows = lambda w: w[::-1].transpose(1, 0, 3, 2).reshape(g, CHUNK_W, p_dim)
    er, ei = to_rows(wr), to_rows(wi)
    pr1 = pr[1:CHUNK + 1].transpose(1, 2, 0)[:, :, :, None]
    pi1 = pi[1:CHUNK + 1].transpose(1, 2, 0)[:, :, :, None]
    cr, ci = c_re.transpose(0, 2, 1)[:, :, None, :], c_im.transpose(0, 2, 1)[:, :, None, :]
    fr = (cr * pr1 - ci * pi1).reshape(g, p_dim, CHUNK_W)
    fi = (cr * pi1 + ci * pr1).reshape(g, p_dim, CHUNK_W)

    z_e = jnp.zeros((g // 2, CHUNK_W, p_dim), _F32)
    e_pair = jnp.concatenate([
        jnp.concatenate([er[0::2], z_e, ei[0::2], z_e], axis=2),
        jnp.concatenate([z_e, er[1::2], z_e, ei[1::2]], axis=2)], axis=1)
    z_f = jnp.zeros((g // 2, p_dim, CHUNK_W), _F32)
    f_pair = jnp.concatenate([
        jnp.concatenate([fr[0::2], z_f], axis=2), jnp.concatenate([z_f, fr[1::2]], axis=2),
        jnp.concatenate([-fi[0::2], z_f], axis=2), jnp.concatenate([z_f, -fi[1::2]], axis=2)], axis=1)

    lanes = lambda v: jnp.concatenate([v[:, 0::2], v[:, 1::2]], axis=-1).transpose(1, 0, 2)
    bigr, bigi = lanes(pr[CHUNK:]), lanes(pi[CHUNK:])
    sub = jnp.arange(SUBLANES)[None, None, :, None]
    dist = jnp.array([1, 2, 4])
    lvl_r = jnp.where(sub >= dist[None, :, None, None], bigr[:, dist - 1, None, :], 0.0)
    lvl_i = jnp.where(sub >= dist[None, :, None, None], bigi[:, dist - 1, None, :], 0.0)
    return (m_mat.astype(_BF16), e_pair.astype(_BF16), f_pair.astype(_BF16), lvl_r, lvl_i, bigr, bigi)


def kernel(x, ln1_g, w_in, b_in, sinks, lam_re, lam_im, log_step, ssm_b_re, ssm_b_im, ssm_c_re, ssm_c_im, ssm_d,
           w_glu, b_glu, g_attn, g_ssm, w_out, ln2_g, w_up, conv_w, conv_b, w_down, lnf_g):
    batch, seq, _ = x.shape
    assert ln1_g.shape[0] == 1, "single-layer trunk"
    assert seq % TOK_TILE == 0 and seq % (CHUNK * SUBLANES) == 0
    n = batch * seq
    x2 = x.reshape(n, D_MODEL)

    wq, wk, wv, wu = (w_in[0][:, :ATTN_WIDTH], w_in[0][:, ATTN_WIDTH:ATTN_WIDTH + KV_COLS],
                      w_in[0][:, ATTN_WIDTH + KV_COLS:ATTN_WIDTH + 2 * KV_COLS], w_in[0][:, ATTN_WIDTH + 2 * KV_COLS:])
    bq, bk, bv, bu = (b_in[0][:ATTN_WIDTH], b_in[0][ATTN_WIDTH:ATTN_WIDTH + KV_COLS],
                      b_in[0][ATTN_WIDTH + KV_COLS:ATTN_WIDTH + 2 * KV_COLS], b_in[0][ATTN_WIDTH + 2 * KV_COLS:])
    dup = lambda t: jnp.concatenate([t[..., :HEAD_DIM], t[..., :HEAD_DIM], t[..., HEAD_DIM:], t[..., HEAD_DIM:]], axis=-1)
    w_ext = jnp.concatenate([wq, dup(wk), dup(wv), wu], axis=1).astype(_BF16)
    b_ext = jnp.concatenate([bq, dup(bk), dup(bv), bu])[None, :]
    inv_freq = ROPE_THETA ** (-jnp.arange(ROT_HALF, dtype=_F32) * 2.0 / ROT_DIM)
    head_lane = jnp.arange(LANES) % HEAD_DIM
    freq = jnp.where(head_lane < ROT_DIM, inv_freq[head_lane % ROT_HALF], 0.0)[None, :]

    q, kd, vd, u_g = _proj_call(x2, ln1_g, w_ext, b_ext, freq, seq)
    mix_a = _attn_call(sinks[0], q, kd, vd, g_attn, batch, seq)

    ops = _ssm_operators(lam_re[0], lam_im[0], log_step[0], ssm_b_re[0], ssm_b_im[0], ssm_c_re[0], ssm_c_im[0], ssm_d[0])
    y_g = _ssm_call(u_g, *ops, seq)
    mix_s = _glu_call(y_g, w_glu[0].astype(_BF16), b_glu, g_ssm)

    out = _ffn_call(x2, mix_a, mix_s, w_out[0].astype(_BF16), ln2_g, w_up[0].astype(_BF16), conv_w[0], conv_b,
                    w_down[0].astype(_BF16), lnf_g[None, :], seq)
    return out.reshape(batch, seq, D_MODEL)
```

```python
import functools
import math

import jax
import jax.numpy as jnp
from jax import lax
from jax.experimental import pallas as pl
from jax.experimental.pallas import tpu as pltpu

D_MODEL = 1024
HEAD_DIM = 64
N_Q_HEADS = 8
N_KV_HEADS = 2
KV_GROUP = N_Q_HEADS // N_KV_HEADS
ATTN_WIDTH = N_Q_HEADS * HEAD_DIM
KV_COLS = N_KV_HEADS * HEAD_DIM
SSM_WIDTH = 512
WINDOW = 128
BLOCK = 128
ROPE_THETA = 500000.0
ROT_DIM = HEAD_DIM // 4
ROT_HALF = ROT_DIM // 2
GROUP_CH = 16
N_GROUPS = SSM_WIDTH // GROUP_CH
N_PAIRS = N_GROUPS // 2
SSM_STATE = 64
D_FF = 2816
CONV_WIDTH = 3
NORM_EPS = 1e-5
MASK_VALUE = -1e30

LANES = 128
SUBLANES = 8
CHUNK = 16
CHUNK_W = CHUNK * GROUP_CH
GROUPS_PER_SLAB = LANES // GROUP_CH
N_SLABS = SSM_WIDTH // LANES
FF_CHUNK = 256
N_FF_CHUNKS = D_FF // FF_CHUNK
VMEM_LIMIT = 56 * 1024 * 1024

TOK_TILE = 512
FFN_TILE = 512

_F32 = jnp.float32
_BF16 = jnp.bfloat16


def _rms(x, g):
    return x * lax.rsqrt(jnp.mean(x * x, axis=-1, keepdims=True) + NORM_EPS) * g


def _dot(a, b):
    return jnp.dot(a, b, preferred_element_type=_F32)


def _transpose_lane_blocks(tiles):
    assert len(tiles) == GROUPS_PER_SLAB
    blk = lax.broadcasted_iota(jnp.int32, tiles[0].shape, 1) // GROUP_CH
    for d in (4, 2, 1):
        keep = (blk & d) == 0
        shift = d * GROUP_CH
        nxt = list(tiles)
        for i in range(GROUPS_PER_SLAB):
            if i & d == 0:
                a, b = tiles[i], tiles[i + d]
                nxt[i] = jnp.where(keep, a, pltpu.roll(b, shift, 1))
                nxt[i + d] = jnp.where(keep, pltpu.roll(a, LANES - shift, 1), b)
        tiles = nxt
    return tiles


def _proj_kernel(x_ref, g_ref, w_ref, b_ref, wvt_ref, bvt_ref, freq_ref, q_ref, k_ref, vt_ref, u_ref, us_ref, cs_ref,
                 *, tiles_per_seq):
    tm = x_ref.shape[0]
    freq = freq_ref[...]

    @pl.when(pl.program_id(0) == 0)
    def _():
        off = lax.broadcasted_iota(jnp.int32, (tm, LANES), 0).astype(_F32) * freq
        cs_ref[0] = jnp.cos(off)
        cs_ref[1] = jnp.sin(off)

    hn = _rms(x_ref[...], g_ref[...]).astype(_BF16)
    proj = _dot(hn, w_ref[...]) + b_ref[...]
    vt = lax.dot_general(wvt_ref[...], hn, (((1,), (1,)), ((), ())), preferred_element_type=_F32)
    vt_ref[...] = (vt + bvt_ref[...]).astype(_BF16)

    base =((pl.program_id(0) % tiles_per_seq) * tm).astype(_F32) * freq
    cos_b, sin_b = jnp.cos(base), jnp.sin(base)
    hd = lax.broadcasted_iota(jnp.int32, (1, LANES), 1) % HEAD_DIM
    lo_dims, hi_dims = hd < ROT_HALF, (hd >= ROT_HALF) & (hd < ROT_DIM)
    cos_o, sin_o = cs_ref[0], cs_ref[1]
    cos = cos_b * cos_o - sin_b * sin_o
    sin_lo = jnp.where(lo_dims, -sin_b, 0.0) * cos_o + jnp.where(lo_dims, -cos_b, 0.0) * sin_o
    sin_hi = jnp.where(hi_dims, sin_b, 0.0) * cos_o + jnp.where(hi_dims, cos_b, 0.0) * sin_o

    def rot(blk):
        return blk * cos + pltpu.roll(blk, LANES - ROT_HALF, 1) * sin_lo + pltpu.roll(blk, ROT_HALF, 1) * sin_hi

    scale = HEAD_DIM ** -0.5 * math.log2(math.e)
    for c in range(ATTN_WIDTH // LANES):
        q_ref[:, c * LANES:(c + 1) * LANES] = (rot(proj[:, c * LANES:(c + 1) * LANES]) * scale).astype(_BF16)
    k0 = ATTN_WIDTH
    for c in range(2 * KV_COLS // LANES):
        k_ref[:, c * LANES:(c + 1) * LANES] = rot(proj[:, k0 + c * LANES:k0 + (c + 1) * LANES]).astype(_BF16)

    u0 = k0 + 2 * KV_COLS
    for j in range(N_SLABS):
        us_ref[j] = proj[:, u0 + j * LANES:u0 + (j + 1) * LANES]
    n_chunks = tm // CHUNK
    for j in range(N_SLABS):
        for h in range(CHUNK // GROUPS_PER_SLAB):
            src = [us_ref[j, pl.ds(h * GROUPS_PER_SLAB + tt, n_chunks, stride=CHUNK), :]
                   for tt in range(GROUPS_PER_SLAB)]
            for gm, tile in enumerate(_transpose_lane_blocks(src)):
                u_ref[j * GROUPS_PER_SLAB + gm, :, h * LANES:(h + 1) * LANES] = tile.astype(_BF16)


def _proj_call(x2, ln1_g, w_ext, b_ext, wvt, bvt, freq, seq):
    n = x2.shape[0]
    tm = TOK_TILE
    nc = w_ext.shape[1]
    const = lambda i: (0, 0)
    return pl.pallas_call(
        functools.partial(_proj_kernel, tiles_per_seq=seq // tm),
        grid=(n // tm,),
        in_specs=[
            pl.BlockSpec((tm, D_MODEL), lambda i: (i, 0)),
            pl.BlockSpec((1, D_MODEL), const),
            pl.BlockSpec((D_MODEL, nc), const),
            pl.BlockSpec((1, nc), const),
            pl.BlockSpec((2 * KV_COLS, D_MODEL), const),
            pl.BlockSpec((2 * KV_COLS, 1), const),
            pl.BlockSpec((1, LANES), const),
        ],
        out_specs=[
            pl.BlockSpec((tm, ATTN_WIDTH), lambda i: (i, 0)),
            pl.BlockSpec((tm, 2 * KV_COLS), lambda i: (i, 0)),
            pl.BlockSpec((2 * KV_COLS, tm), lambda i: (0, i)),
            pl.BlockSpec((N_GROUPS, tm // CHUNK, CHUNK_W), lambda i: (0, i, 0)),
        ],
        out_shape=[
            jax.ShapeDtypeStruct((n, ATTN_WIDTH), _BF16),
            jax.ShapeDtypeStruct((n, 2 * KV_COLS), _BF16),
            jax.ShapeDtypeStruct((2 * KV_COLS, n), _BF16),
            jax.ShapeDtypeStruct((N_GROUPS, n // CHUNK, CHUNK_W), _BF16),
        ],
        scratch_shapes=[pltpu.VMEM((N_SLABS, tm, LANES), _F32), pltpu.VMEM((2, tm, LANES), _F32)],
        compiler_params=pltpu.CompilerParams(dimension_semantics=("arbitrary",), vmem_limit_bytes=VMEM_LIMIT),
        name="proj",
    )(x2, ln1_g, w_ext, b_ext, wvt, bvt, freq)


def _attn_kernel(sink_ref, q_ref, kc_ref, kp_ref, vc_ref, vp_ref, g_ref, o_ref, cap_ref):
    tq = q_ref.shape[0]
    first_tile = pl.program_id(1) == 0
    lo = lax.broadcasted_iota(jnp.int32, (BLOCK, LANES), 1) < HEAD_DIM
    zero = jnp.zeros((BLOCK, LANES), _BF16)
    kj = lax.broadcasted_iota(jnp.int32, (2 * BLOCK, BLOCK), 0)
    qi = lax.broadcasted_iota(jnp.int32, (2 * BLOCK, BLOCK), 1)
    band = (kj > qi) & (kj <= qi + WINDOW)
    no_cap = float(jnp.finfo(jnp.float32).max)
    cap_ref[0] = jnp.where(band, no_cap, MASK_VALUE)
    cap_ref[1] = jnp.where(band & (kj >= BLOCK), no_cap, MASK_VALUE)
    units = [(j, hk) for j in range(tq // BLOCK) for hk in range(N_KV_HEADS)]

    def scores(j, hk):
        rows = slice(j * BLOCK, (j + 1) * BLOCK)
        kv = slice(hk * LANES, (hk + 1) * LANES)
        if j == 0:
            kcat = jnp.concatenate([kp_ref[:, kv], kc_ref[rows, kv]], axis=0)
        else:
            kcat = kc_ref[(j - 1) * BLOCK:(j + 1) * BLOCK, kv]
        qa = q_ref[rows, (2 * hk) * LANES:(2 * hk + 1) * LANES]
        qb = q_ref[rows, (2 * hk + 1) * LANES:(2 * hk + 2) * LANES]
        qs = jnp.concatenate([jnp.where(lo, qa, zero), jnp.where(lo, zero, qa),
                              jnp.where(lo, qb, zero), jnp.where(lo, zero, qb)], axis=0)
        return lax.dot_general(kcat, qs, (((1,), (1,)), ((), ())), preferred_element_type=_F32)

    def attend(j, hk, s):
        kv = slice(hk * LANES, (hk + 1) * LANES)
        cap = cap_ref[jnp.where(first_tile, 1, 0)] if j == 0 else cap_ref[0]
        if j == 0:
            vcat = jnp.concatenate([vp_ref[kv, :], vc_ref[kv, 0:BLOCK]], axis=1)
        else:
            vcat = vc_ref[kv, (j - 1) * BLOCK:(j + 1) * BLOCK]
        probs, rest = [], []
        for a in range(KV_GROUP):
            sa = jnp.minimum(s[:, a * BLOCK:(a + 1) * BLOCK], cap)
            sink = sink_ref[KV_GROUP * hk + a]
            m = jnp.maximum(jnp.max(sa, axis=0, keepdims=True), sink)
            probs.append(jnp.exp2(sa - m).astype(_BF16))
            rest.append(jnp.exp2(sink - m))
        o = _dot(vcat, jnp.concatenate(probs, axis=1))
        o = o / (o[HEAD_DIM:HEAD_DIM + 1, :] + jnp.concatenate(rest, axis=1))
        blk = [o[:, a * BLOCK:(a + 1) * BLOCK].T for a in range(KV_GROUP)]
        return [jnp.where(lo, blk[0], pltpu.roll(blk[1], HEAD_DIM, 1)),
                jnp.where(lo, blk[2], pltpu.roll(blk[3], HEAD_DIM, 1))]

    s_next = scores(*units[0])
    cols = []
    for n, (j, hk) in enumerate(units):
        s_cur = s_next
        if n + 1 < len(units):
            s_next = scores(*units[n + 1])
        cols += attend(j, hk, s_cur)
        if hk == N_KV_HEADS - 1:
            attn = jnp.concatenate(cols, axis=1)
            o_ref[j * BLOCK:(j + 1) * BLOCK, :] = _rms(attn, g_ref[...]).astype(_BF16)
            cols = []


def _attn_call(sinks, q, kd, vt, g_attn, batch, seq):
    tq = TOK_TILE
    nt = seq // tq
    bpt = tq // BLOCK
    cur = lambda b, i, *_: (b * nt + i, 0)
    prev = lambda b, i, *_: (jnp.maximum((b * nt + i) * bpt - 1, 0), 0)
    cur_t = lambda b, i, *_: (0, b * nt + i)
    prev_t = lambda b, i, *_: (0, jnp.maximum((b * nt + i) * bpt - 1, 0))
    grid_spec = pltpu.PrefetchScalarGridSpec(
        num_scalar_prefetch=1,
        grid=(batch, nt),
        in_specs=[
            pl.BlockSpec((tq, ATTN_WIDTH), cur),
            pl.BlockSpec((tq, 2 * KV_COLS), cur),
            pl.BlockSpec((BLOCK, 2 * KV_COLS), prev),
            pl.BlockSpec((2 * KV_COLS, tq), cur_t),
            pl.BlockSpec((2 * KV_COLS, BLOCK), prev_t),
            pl.BlockSpec((1, ATTN_WIDTH), lambda b, i, *_: (0, 0)),
        ],
        out_specs=pl.BlockSpec((tq, ATTN_WIDTH), cur),
        scratch_shapes=[pltpu.VMEM((2, 2 * BLOCK, BLOCK), _F32)],
    )
    return pl.pallas_call(
        _attn_kernel,
        grid_spec=grid_spec,
        out_shape=jax.ShapeDtypeStruct(q.shape, _BF16),
        compiler_params=pltpu.CompilerParams(dimension_semantics=("arbitrary", "arbitrary"),
                                             vmem_limit_bytes=VMEM_LIMIT),
        name="attn",
    )(sinks, q, kd, kd, vt, vt, g_attn)


def _dot_f32(a, b):
    return jnp.dot(a, b, preferred_element_type=_F32, precision=lax.Precision.HIGHEST)


def _shift_lanes(x, k):
    lo, hi = x[:, :LANES], x[:, LANES:]
    zero = jnp.zeros_like(lo)
    if k == 0:
        return x
    if k == LANES:
        return jnp.concatenate([zero, lo], axis=1)
    lane = lax.broadcasted_iota(jnp.int32, lo.shape, 1)
    if k < LANES:
        rl, rh = pltpu.roll(lo, k, 1), pltpu.roll(hi, k, 1)
        return jnp.concatenate([jnp.where(lane < k, zero, rl), jnp.where(lane < k, rl, rh)], axis=1)
    rl = pltpu.roll(lo, k - LANES, 1)
    return jnp.concatenate([zero, jnp.where(lane < k - LANES, zero, rl)], axis=1)


def _ssm_operators(col_ref, row_ref, bt_ref, ct_ref, dk_ref, m_ref, e_ref, f_ref):
    lr_c, li_c, dt_c = col_ref[0, 0], col_ref[0, 1], jnp.exp(col_ref[0, 2])
    lr_r, li_r, dt_r = row_ref[0, 0], row_ref[0, 1], jnp.exp(row_ref[0, 2])
    zr_c, zi_c, zr_r, zi_r = lr_c * dt_c, li_c * dt_c, lr_r * dt_r, li_r * dt_r

    tl = (lax.broadcasted_iota(jnp.int32, (LANES, CHUNK_W), 1) // GROUP_CH).astype(_F32)
    mag = jnp.exp(zr_c * tl)
    pr0, pi0 = mag * jnp.cos(zi_c * tl), mag * jnp.sin(zi_c * tl)
    ar_c, ai_c = jnp.exp(zr_c) * jnp.cos(zi_c), jnp.exp(zr_c) * jnp.sin(zi_c)
    pr1, pi1 = pr0 * ar_c - pi0 * ai_c, pr0 * ai_c + pi0 * ar_c
    ctr, cti = ct_ref[0, 0], ct_ref[0, 1]
    w0r, w0i = pr0 * ctr - pi0 * cti, pr0 * cti + pi0 * ctr
    w1r, w1i = pr1 * ctr - pi1 * cti, pr1 * cti + pi1 * ctr
    top = lax.broadcasted_iota(jnp.int32, (LANES, CHUNK_W), 0) < SSM_STATE
    f_ref[0:LANES, 0:CHUNK_W] = jnp.where(top, w1r, 0.0)
    f_ref[0:LANES, CHUNK_W:] = jnp.where(top, 0.0, w1r)
    f_ref[LANES:, 0:CHUNK_W] = jnp.where(top, -w1i, 0.0)
    f_ref[LANES:, CHUNK_W:] = jnp.where(top, 0.0, -w1i)

    mag_r = jnp.exp(zr_r)
    ar_r, ai_r = mag_r * jnp.cos(zi_r), mag_r * jnp.sin(zi_r)
    den = lr_r * lr_r + li_r * li_r
    qr = ((ar_r - 1.0) * lr_r + ai_r * li_r) / den
    qi = (ai_r * lr_r - (ar_r - 1.0) * li_r) / den
    btr, bti = bt_ref[0, 0], bt_ref[0, 1]
    bbr, bbi = qr * btr - qi * bti, qr * bti + qi * btr
    n_row = lax.broadcasted_iota(jnp.int32, (CHUNK, LANES), 0).astype(_F32)
    mag_n = jnp.exp(zr_r * n_row)
    prn, pin = mag_n * jnp.cos(zi_r * n_row), mag_n * jnp.sin(zi_r * n_row)
    first = lax.broadcasted_iota(jnp.int32, (GROUP_CH, LANES), 1) < SSM_STATE
    diag = (lax.broadcasted_iota(jnp.int32, (CHUNK_W, CHUNK_W), 0)
            == lax.broadcasted_iota(jnp.int32, (CHUNK_W, CHUNK_W), 1))
    for a in range(2):
        mine = first if a == 0 else jnp.logical_not(first)
        br, bi = jnp.where(mine, bbr, 0.0), jnp.where(mine, bbi, 0.0)
        m0 = _dot_f32(br, w0r) - _dot_f32(bi, w0i)
        for s in range(CHUNK):
            m_ref[a, s * GROUP_CH:(s + 1) * GROUP_CH, :] = _shift_lanes(m0, s * GROUP_CH)
            pn_r, pn_i = prn[CHUNK - 1 - s:CHUNK - s, :], pin[CHUNK - 1 - s:CHUNK - s, :]
            rows = slice(a * CHUNK_W + s * GROUP_CH, a * CHUNK_W + (s + 1) * GROUP_CH)
            e_ref[rows, 0:LANES] = br * pn_r - bi * pn_i
            e_ref[rows, LANES:] = br * pn_i + bi * pn_r
        m_ref[a] = m_ref[a] + jnp.where(diag, dk_ref[0, a], 0.0)

    k_row = ((lax.broadcasted_iota(jnp.int32, (SUBLANES, LANES), 0) + 1) * CHUNK).astype(_F32)
    mag_k = jnp.exp(zr_r * k_row)
    return mag_k * jnp.cos(zi_r * k_row), mag_k * jnp.sin(zi_r * k_row)


def _ssm_kernel(u_ref, col_ref, row_ref, bt_ref, ct_ref, dk_ref, y_ref, m_ref, e_ref, f_ref, sr_ref, si_ref, xp_ref,
                *, chunks_per_seq):
    n_chunks = u_ref.shape[1]
    big_r, big_i = _ssm_operators(col_ref, row_ref, bt_ref, ct_ref, dk_ref, m_ref, e_ref, f_ref)
    u0, u1 = u_ref[0], u_ref[1]
    s = (_dot(u0, e_ref[0:CHUNK_W, :].astype(_BF16)) + _dot(u1, e_ref[CHUNK_W:, :].astype(_BF16)))
    sr_ref[...] = s[:, 0:LANES]
    si_ref[...] = s[:, LANES:2 * LANES]

    rows_per_seq = chunks_per_seq // SUBLANES
    sub = lax.broadcasted_iota(jnp.int32, (SUBLANES, LANES), 0)
    levels = []
    for d in (1, 2, 4):
        levels.append((d, jnp.where(sub >= d, jnp.broadcast_to(big_r[d - 1:d, :], (SUBLANES, LANES)), 0.0),
                       jnp.where(sub >= d, jnp.broadcast_to(big_i[d - 1:d, :], (SUBLANES, LANES)), 0.0)))

    def step(r, carry):
        cre, cim = carry
        fresh = (r % rows_per_seq) == 0
        cre = jnp.where(fresh, 0.0, cre)
        cim = jnp.where(fresh, 0.0, cim)
        base = pl.multiple_of(r * SUBLANES, SUBLANES)
        vr, vi = sr_ref[pl.ds(base, SUBLANES), :], si_ref[pl.ds(base, SUBLANES), :]
        for d, ar, ai in levels:
            wr, wi = pltpu.roll(vr, d, 0), pltpu.roll(vi, d, 0)
            vr, vi = vr + ar * wr - ai * wi, vi + ar * wi + ai * wr
        pr, pi = big_r, big_i
        xr = vr + pr * cre - pi * cim
        xi = vi + pr * cim + pi * cre
        prev_r = jnp.where(sub == 0, cre, pltpu.roll(xr, 1, 0))
        prev_i = jnp.where(sub == 0, cim, pltpu.roll(xi, 1, 0))
        xp_ref[pl.ds(base, SUBLANES), 0:LANES] = prev_r
        xp_ref[pl.ds(base, SUBLANES), LANES:2 * LANES] = prev_i
        last_r = jnp.broadcast_to(xr[SUBLANES - 1:SUBLANES, :], (SUBLANES, LANES))
        last_i = jnp.broadcast_to(xi[SUBLANES - 1:SUBLANES, :], (SUBLANES, LANES))
        return last_r, last_i

    zeros = jnp.zeros((SUBLANES, LANES), _F32)
    lax.fori_loop(0, n_chunks // SUBLANES, step, (zeros, zeros))

    xp = xp_ref[...].astype(_BF16)
    for a, ua in enumerate((u0, u1)):
        y = (_dot(ua, m_ref[a].astype(_BF16))
             + _dot(xp, f_ref[:, a * CHUNK_W:(a + 1) * CHUNK_W].astype(_BF16)))
        y_ref[a] = jax.nn.gelu(y).astype(_BF16)


def _ssm_call(u_g, lam_re, lam_im, log_step, b_re, b_im, c_re, c_im, d_skip, seq):
    n_chunks = u_g.shape[1]
    g, p_dim = lam_re.shape
    rows = jnp.stack([lam_re, lam_im, jnp.broadcast_to(log_step[:, None], (g, p_dim))], axis=0)
    rows = rows.reshape(3, g // 2, 1, 2 * p_dim).transpose(1, 0, 2, 3)
    cols = rows.transpose(0, 1, 3, 2)
    bt = jnp.stack([b_re, b_im], axis=0).reshape(2, g // 2, 2, p_dim, GROUP_CH)
    bt = bt.transpose(1, 0, 4, 2, 3).reshape(g // 2, 2, GROUP_CH, 2 * p_dim)
    ct = jnp.stack([c_re, c_im], axis=0).transpose(1, 0, 3, 2)
    ct = jnp.tile(ct, (1, 1, 1, CHUNK)).reshape(g // 2, 2, 2, p_dim, CHUNK_W)
    ct = ct.transpose(0, 2, 1, 3, 4).reshape(g // 2, 2, 2 * p_dim, CHUNK_W)
    dk = jnp.tile(d_skip, (1, CHUNK)).reshape(g // 2, 2, 1, CHUNK_W)
    pair3 = lambda i: (i, 0, 0)
    pair4 = lambda i: (i, 0, 0, 0)
    return pl.pallas_call(
        functools.partial(_ssm_kernel, chunks_per_seq=seq // CHUNK),
        grid=(N_PAIRS,),
        in_specs=[
            pl.BlockSpec((2, n_chunks, CHUNK_W), pair3),
            pl.BlockSpec((1, 3, 2 * p_dim, 1), pair4),
            pl.BlockSpec((1, 3, 1, 2 * p_dim), pair4),
            pl.BlockSpec((1, 2, GROUP_CH, 2 * p_dim), pair4),
            pl.BlockSpec((1, 2, 2 * p_dim, CHUNK_W), pair4),
            pl.BlockSpec((1, 2, 1, CHUNK_W), pair4),
        ],
        out_specs=pl.BlockSpec((2, n_chunks, CHUNK_W), pair3),
        out_shape=jax.ShapeDtypeStruct(u_g.shape, _BF16),
        scratch_shapes=[
            pltpu.VMEM((2, CHUNK_W, CHUNK_W), _F32),
            pltpu.VMEM((2 * CHUNK_W, 2 * LANES), _F32),
            pltpu.VMEM((2 * LANES, 2 * CHUNK_W), _F32),
            pltpu.VMEM((n_chunks, LANES), _F32),
            pltpu.VMEM((n_chunks, LANES), _F32),
            pltpu.VMEM((n_chunks, 2 * LANES), _F32),
        ],
        compiler_params=pltpu.CompilerParams(dimension_semantics=("arbitrary",), vmem_limit_bytes=VMEM_LIMIT),
        name="ssm",
    )(u_g, cols, rows, bt, ct, dk)


def _glu_kernel(y_ref, w_ref, b_ref, g_ref, o_ref, ys_ref):
    n_chunks = y_ref.shape[1]
    for j in range(N_SLABS):
        for h in range(CHUNK // GROUPS_PER_SLAB):
            src = [y_ref[j * GROUPS_PER_SLAB + gm, :, h * LANES:(h + 1) * LANES].astype(_F32)
                   for gm in range(GROUPS_PER_SLAB)]
            for tt, tile in enumerate(_transpose_lane_blocks(src)):
                ys_ref[j, pl.ds(h * GROUPS_PER_SLAB + tt, n_chunks, stride=CHUNK), :] = tile
    y = jnp.concatenate([ys_ref[j] for j in range(N_SLABS)], axis=1)
    z = _dot(y.astype(_BF16), w_ref[...]) + b_ref[...]
    o_ref[...] = _rms(y * jax.nn.sigmoid(z), g_ref[...]).astype(_BF16)


def _glu_call(y_g, w_glu, b_glu, g_ssm):
    n = y_g.shape[1] * CHUNK
    tm = TOK_TILE
    const = lambda i: (0, 0)
    return pl.pallas_call(
        _glu_kernel,
        grid=(n // tm,),
        in_specs=[
            pl.BlockSpec((N_GROUPS, tm // CHUNK, CHUNK_W), lambda i: (0, i, 0)),
            pl.BlockSpec((SSM_WIDTH, SSM_WIDTH), const),
            pl.BlockSpec((1, SSM_WIDTH), const),
            pl.BlockSpec((1, SSM_WIDTH), const),
        ],
        out_specs=pl.BlockSpec((tm, SSM_WIDTH), lambda i: (i, 0)),
        out_shape=jax.ShapeDtypeStruct((n, SSM_WIDTH), _BF16),
        scratch_shapes=[pltpu.VMEM((N_SLABS, tm, LANES), _F32)],
        compiler_params=pltpu.CompilerParams(dimension_semantics=("arbitrary",), vmem_limit_bytes=VMEM_LIMIT),
        name="glu",
    )(y_g, w_glu, b_glu, g_ssm)


def _ffn_kernel(x_ref, ma_ref, ms_ref, wo_ref, g2_ref, wu_ref, cw_ref, cb_ref, wd_ref, gf_ref, o_ref,
                carry_ref, act_ref, hn_ref, *, tiles_per_seq):
    tm = x_ref.shape[0]

    @pl.when((pl.program_id(0) % tiles_per_seq) == 0)
    def _():
        carry_ref[...] = jnp.zeros(carry_ref.shape, _F32)

    h1 = x_ref[...] + _dot(ma_ref[...], wo_ref[0:ATTN_WIDTH, :]) + _dot(ms_ref[...], wo_ref[ATTN_WIDTH:, :])
    hn_ref[...] = _rms(h1, g2_ref[...]).astype(_BF16)

    def conv_part(c):
        cols = slice(c * FF_CHUNK, (c + 1) * FF_CHUNK)
        up = _dot(hn_ref[...], wu_ref[:, cols])
        ext = jnp.concatenate([carry_ref[:, cols], up], axis=0)
        carry_ref[:, cols] = up[tm - SUBLANES:, :]
        prev1 = pltpu.roll(ext, 1, 0)[SUBLANES:, :]
        prev2 = pltpu.roll(ext, 2, 0)[SUBLANES:, :]
        return (cw_ref[2:3, cols] * up + cw_ref[1:2, cols] * prev1 + cw_ref[0:1, cols] * prev2 + cb_ref[:, cols])

    for j in range(N_FF_CHUNKS):
        gate = conv_part(j)
        val = conv_part(j + N_FF_CHUNKS)
        act_ref[:, j * FF_CHUNK:(j + 1) * FF_CHUNK] = (jax.nn.silu(gate) * val).astype(_BF16)

    o_ref[...] = _rms(h1 + _dot(act_ref[...], wd_ref[...]), gf_ref[...])


def _ffn_call(x2, mix_a, mix_s, w_out, ln2_g, w_up, conv_w, conv_b, w_down, lnf_g, seq):
    n = x2.shape[0]
    tm = FFN_TILE
    const = lambda i: (0, 0)
    once = pl.Buffered(1)
    return pl.pallas_call(
        functools.partial(_ffn_kernel, tiles_per_seq=seq // tm),
        grid=(n // tm,),
        in_specs=[
            pl.BlockSpec((tm, D_MODEL), lambda i: (i, 0)),
            pl.BlockSpec((tm, ATTN_WIDTH), lambda i: (i, 0)),
            pl.BlockSpec((tm, SSM_WIDTH), lambda i: (i, 0)),
            pl.BlockSpec((D_MODEL, D_MODEL), const, pipeline_mode=once),
            pl.BlockSpec((1, D_MODEL), const),
            pl.BlockSpec((D_MODEL, 2 * D_FF), const, pipeline_mode=once),
            pl.BlockSpec((CONV_WIDTH, 2 * D_FF), const),
            pl.BlockSpec((1, 2 * D_FF), const),
            pl.BlockSpec((D_FF, D_MODEL), const, pipeline_mode=once),
            pl.BlockSpec((1, D_MODEL), const),
        ],
        out_specs=pl.BlockSpec((tm, D_MODEL), lambda i: (i, 0)),
        out_shape=jax.ShapeDtypeStruct((n, D_MODEL), _F32),
        scratch_shapes=[
            pltpu.VMEM((SUBLANES, 2 * D_FF), _F32),
            pltpu.VMEM((tm, D_FF), _BF16),
            pltpu.VMEM((tm, D_MODEL), _BF16),
        ],
        compiler_params=pltpu.CompilerParams(dimension_semantics=("arbitrary",), vmem_limit_bytes=VMEM_LIMIT),
        name="ffn",
    )(x2, mix_a, mix_s, w_out, ln2_g, w_up, conv_w, conv_b, w_down, lnf_g)


def kernel(x, ln1_g, w_in, b_in, sinks, lam_re, lam_im, log_step, ssm_b_re, ssm_b_im, ssm_c_re, ssm_c_im, ssm_d,
           w_glu, b_glu, g_attn, g_ssm, w_out, ln2_g, w_up, conv_w, conv_b, w_down, lnf_g):
    batch, seq, _ = x.shape
    assert ln1_g.shape[0] == 1, "single-layer trunk"
    assert seq % TOK_TILE == 0 and seq % (CHUNK * SUBLANES) == 0
    n = batch * seq
    x2 = x.reshape(n, D_MODEL)

    wq, wk, wv, wu = (w_in[0][:, :ATTN_WIDTH], w_in[0][:, ATTN_WIDTH:ATTN_WIDTH + KV_COLS],
                      w_in[0][:, ATTN_WIDTH + KV_COLS:ATTN_WIDTH + 2 * KV_COLS], w_in[0][:, ATTN_WIDTH + 2 * KV_COLS:])
    bq, bk, bv, bu = (b_in[0][:ATTN_WIDTH], b_in[0][ATTN_WIDTH:ATTN_WIDTH + KV_COLS],
                      b_in[0][ATTN_WIDTH + KV_COLS:ATTN_WIDTH + 2 * KV_COLS], b_in[0][ATTN_WIDTH + 2 * KV_COLS:])
    dup = lambda t: jnp.concatenate([t[..., :HEAD_DIM], t[..., :HEAD_DIM], t[..., HEAD_DIM:], t[..., HEAD_DIM:]], axis=-1)
    w_ext = jnp.concatenate([wq, dup(wk), wu], axis=1).astype(_BF16)
    b_ext = jnp.concatenate([bq, dup(bk), bu])[None, :]
    zw, ones = jnp.zeros((D_MODEL, HEAD_DIM), _F32), jnp.ones((HEAD_DIM,), _F32)
    wvt = jnp.concatenate([wv[:, :HEAD_DIM], zw, wv[:, HEAD_DIM:], zw], axis=1).T.astype(_BF16)
    bvt = jnp.concatenate([bv[:HEAD_DIM], ones, bv[HEAD_DIM:], ones])[:, None]
    inv_freq = ROPE_THETA ** (-jnp.arange(ROT_HALF, dtype=_F32) * 2.0 / ROT_DIM)
    head_lane = jnp.arange(LANES) % HEAD_DIM
    freq = jnp.where(head_lane < ROT_DIM, inv_freq[head_lane % ROT_HALF], 0.0)[None, :]

    q, kd, vt, u_g = _proj_call(x2, ln1_g, w_ext, b_ext, wvt, bvt, freq, seq)
    mix_a = _attn_call(sinks[0] * math.log2(math.e), q, kd, vt, g_attn, batch, seq)

    y_g = _ssm_call(u_g, lam_re[0], lam_im[0], log_step[0], ssm_b_re[0], ssm_b_im[0], ssm_c_re[0], ssm_c_im[0],
                    ssm_d[0], seq)
    mix_s = _glu_call(y_g, w_glu[0].astype(_BF16), b_glu, g_ssm)

    out = _ffn_call(x2, mix_a, mix_s, w_out[0].astype(_BF16), ln2_g, w_up[0].astype(_BF16), conv_w[0], conv_b,
                    w_down[0].astype(_BF16), lnf_g[None, :], seq)
    return out.reshape(batch, seq, D_MODEL)
```

```python
import functools
import math

import jax
import jax.numpy as jnp
from jax import lax
from jax.experimental import pallas as pl
from jax.experimental.pallas import tpu as pltpu

D_MODEL = 1024
HEAD_DIM = 64
N_Q_HEADS = 8
N_KV_HEADS = 2
KV_GROUP = N_Q_HEADS // N_KV_HEADS
ATTN_WIDTH = N_Q_HEADS * HEAD_DIM
KV_COLS = N_KV_HEADS * HEAD_DIM
SSM_WIDTH = 512
WINDOW = 128
BLOCK = 128
ROPE_THETA = 500000.0
ROT_DIM = HEAD_DIM // 4
ROT_HALF = ROT_DIM // 2
GROUP_CH = 16
N_GROUPS = SSM_WIDTH // GROUP_CH
N_PAIRS = N_GROUPS // 2
SSM_STATE = 64
D_FF = 2816
CONV_WIDTH = 3
NORM_EPS = 1e-5
MASK_VALUE = -1e30

LANES = 128
SUBLANES = 8
CHUNK = 16
CHUNK_W = CHUNK * GROUP_CH
GROUPS_PER_SLAB = LANES // GROUP_CH
N_SLABS = SSM_WIDTH // LANES
FF_CHUNK = 256
N_FF_CHUNKS = D_FF // FF_CHUNK
VMEM_LIMIT = 56 * 1024 * 1024

TOK_TILE = 512
FFN_TILE = 512

_F32 = jnp.float32
_BF16 = jnp.bfloat16


def _rms(x, g):
    return x * lax.rsqrt(jnp.mean(x * x, axis=-1, keepdims=True) + NORM_EPS) * g


def _dot(a, b):
    return jnp.dot(a, b, preferred_element_type=_F32)


def _transpose_lane_blocks(tiles):
    assert len(tiles) == GROUPS_PER_SLAB
    blk = lax.broadcasted_iota(jnp.int32, tiles[0].shape, 1) // GROUP_CH
    for d in (4, 2, 1):
        keep = (blk & d) == 0
        shift = d * GROUP_CH
        nxt = list(tiles)
        for i in range(GROUPS_PER_SLAB):
            if i & d == 0:
                a, b = tiles[i], tiles[i + d]
                nxt[i] = jnp.where(keep, a, pltpu.roll(b, shift, 1))
                nxt[i + d] = jnp.where(keep, pltpu.roll(a, LANES - shift, 1), b)
        tiles = nxt
    return tiles


def _proj_kernel(x_ref, g_ref, w_ref, b_ref, wvt_ref, bvt_ref, freq_ref, q_ref, k_ref, vt_ref, u_ref, us_ref, cs_ref,
                 *, tiles_per_seq):
    tm = x_ref.shape[0]
    freq = freq_ref[...]

    @pl.when(pl.program_id(0) == 0)
    def _():
        off = lax.broadcasted_iota(jnp.int32, (tm, LANES), 0).astype(_F32) * freq
        cs_ref[0] = jnp.cos(off)
        cs_ref[1] = jnp.sin(off)

    hn = _rms(x_ref[...], g_ref[...]).astype(_BF16)
    proj = _dot(hn, w_ref[...]) + b_ref[...]
    vt = lax.dot_general(wvt_ref[...], hn, (((1,), (1,)), ((), ())), preferred_element_type=_F32)
    vt_ref[...] = (vt + bvt_ref[...]).astype(_BF16)

    base =((pl.program_id(0) % tiles_per_seq) * tm).astype(_F32) * freq
    cos_b, sin_b = jnp.cos(base), jnp.sin(base)
    hd = lax.broadcasted_iota(jnp.int32, (1, LANES), 1) % HEAD_DIM
    lo_dims, hi_dims = hd < ROT_HALF, (hd >= ROT_HALF) & (hd < ROT_DIM)
    cos_o, sin_o = cs_ref[0], cs_ref[1]
    cos = cos_b * cos_o - sin_b * sin_o
    sin_lo = jnp.where(lo_dims, -sin_b, 0.0) * cos_o + jnp.where(lo_dims, -cos_b, 0.0) * sin_o
    sin_hi = jnp.where(hi_dims, sin_b, 0.0) * cos_o + jnp.where(hi_dims, cos_b, 0.0) * sin_o

    def rot(blk):
        return blk * cos + pltpu.roll(blk, LANES - ROT_HALF, 1) * sin_lo + pltpu.roll(blk, ROT_HALF, 1) * sin_hi

    scale = HEAD_DIM ** -0.5 * math.log2(math.e)
    for c in range(ATTN_WIDTH // LANES):
        q_ref[:, c * LANES:(c + 1) * LANES] = (rot(proj[:, c * LANES:(c + 1) * LANES]) * scale).astype(_BF16)
    k0 = ATTN_WIDTH
    for c in range(2 * KV_COLS // LANES):
        k_ref[:, c * LANES:(c + 1) * LANES] = rot(proj[:, k0 + c * LANES:k0 + (c + 1) * LANES]).astype(_BF16)

    u0 = k0 + 2 * KV_COLS
    for j in range(N_SLABS):
        us_ref[j] = proj[:, u0 + j * LANES:u0 + (j + 1) * LANES]
    n_chunks = tm // CHUNK
    for j in range(N_SLABS):
        for h in range(CHUNK // GROUPS_PER_SLAB):
            src = [us_ref[j, pl.ds(h * GROUPS_PER_SLAB + tt, n_chunks, stride=CHUNK), :]
                   for tt in range(GROUPS_PER_SLAB)]
            for gm, tile in enumerate(_transpose_lane_blocks(src)):
                u_ref[j * GROUPS_PER_SLAB + gm, :, h * LANES:(h + 1) * LANES] = tile.astype(_BF16)


def _proj_call(x2, ln1_g, w_ext, b_ext, wvt, bvt, freq, seq):
    n = x2.shape[0]
    tm = TOK_TILE
    nc = w_ext.shape[1]
    const = lambda i: (0, 0)
    return pl.pallas_call(
        functools.partial(_proj_kernel, tiles_per_seq=seq // tm),
        grid=(n // tm,),
        in_specs=[
            pl.BlockSpec((tm, D_MODEL), lambda i: (i, 0)),
            pl.BlockSpec((1, D_MODEL), const),
            pl.BlockSpec((D_MODEL, nc), const),
            pl.BlockSpec((1, nc), const),
            pl.BlockSpec((2 * KV_COLS, D_MODEL), const),
            pl.BlockSpec((2 * KV_COLS, 1), const),
            pl.BlockSpec((1, LANES), const),
        ],
        out_specs=[
            pl.BlockSpec((tm, ATTN_WIDTH), lambda i: (i, 0)),
            pl.BlockSpec((tm, 2 * KV_COLS), lambda i: (i, 0)),
            pl.BlockSpec((2 * KV_COLS, tm), lambda i: (0, i)),
            pl.BlockSpec((N_GROUPS, tm // CHUNK, CHUNK_W), lambda i: (0, i, 0)),
        ],
        out_shape=[
            jax.ShapeDtypeStruct((n, ATTN_WIDTH), _BF16),
            jax.ShapeDtypeStruct((n, 2 * KV_COLS), _BF16),
            jax.ShapeDtypeStruct((2 * KV_COLS, n), _BF16),
            jax.ShapeDtypeStruct((N_GROUPS, n // CHUNK, CHUNK_W), _BF16),
        ],
        scratch_shapes=[pltpu.VMEM((N_SLABS, tm, LANES), _F32), pltpu.VMEM((2, tm, LANES), _F32)],
        compiler_params=pltpu.CompilerParams(dimension_semantics=("arbitrary",), vmem_limit_bytes=VMEM_LIMIT),
        name="proj",
    )(x2, ln1_g, w_ext, b_ext, wvt, bvt, freq)


def _attn_kernel(sink_ref, q_ref, kc_ref, kp_ref, vc_ref, vp_ref, g_ref, o_ref, cap_ref):
    tq = q_ref.shape[0]
    first_tile = pl.program_id(1) == 0
    lo = lax.broadcasted_iota(jnp.int32, (BLOCK, LANES), 1) < HEAD_DIM
    zero = jnp.zeros((BLOCK, LANES), _BF16)

    @pl.when((pl.program_id(0) == 0) & first_tile)
    def _():
        kj = lax.broadcasted_iota(jnp.int32, (2 * BLOCK, BLOCK), 0)
        qi = lax.broadcasted_iota(jnp.int32, (2 * BLOCK, BLOCK), 1)
        band = (kj > qi) & (kj <= qi + WINDOW)
        no_cap = float(jnp.finfo(jnp.float32).max)
        cap_ref[0] = jnp.where(band, no_cap, MASK_VALUE)
        cap_ref[1] = jnp.where(band & (kj >= BLOCK), no_cap, MASK_VALUE)

    units = [(j, hk) for j in range(tq // BLOCK) for hk in range(N_KV_HEADS)]

    def scores(j, hk):
        rows = slice(j * BLOCK, (j + 1) * BLOCK)
        kv = slice(hk * LANES, (hk + 1) * LANES)
        if j == 0:
            kcat = jnp.concatenate([kp_ref[:, kv], kc_ref[rows, kv]], axis=0)
        else:
            kcat = kc_ref[(j - 1) * BLOCK:(j + 1) * BLOCK, kv]
        qa = q_ref[rows, (2 * hk) * LANES:(2 * hk + 1) * LANES]
        qb = q_ref[rows, (2 * hk + 1) * LANES:(2 * hk + 2) * LANES]
        qs = jnp.concatenate([jnp.where(lo, qa, zero), jnp.where(lo, zero, qa),
                              jnp.where(lo, qb, zero), jnp.where(lo, zero, qb)], axis=0)
        return lax.dot_general(kcat, qs, (((1,), (1,)), ((), ())), preferred_element_type=_F32)

    def attend(j, hk, s):
        kv = slice(hk * LANES, (hk + 1) * LANES)
        cap = cap_ref[jnp.where(first_tile, 1, 0)] if j == 0 else cap_ref[0]
        if j == 0:
            vcat = jnp.concatenate([vp_ref[kv, :], vc_ref[kv, 0:BLOCK]], axis=1)
        else:
            vcat = vc_ref[kv, (j - 1) * BLOCK:(j + 1) * BLOCK]
        probs, rest = [], []
        for a in range(KV_GROUP):
            sa = jnp.minimum(s[:, a * BLOCK:(a + 1) * BLOCK], cap)
            sink = sink_ref[KV_GROUP * hk + a]
            m = jnp.maximum(jnp.max(sa, axis=0, keepdims=True), sink)
            probs.append(jnp.exp2(sa - m).astype(_BF16))
            rest.append(jnp.exp2(sink - m))
        o = _dot(vcat, jnp.concatenate(probs, axis=1))
        o = o[0:HEAD_DIM, :] / (o[HEAD_DIM:HEAD_DIM + 1, :] + jnp.concatenate(rest, axis=1))
        return [o[:, a * BLOCK:(a + 1) * BLOCK] for a in range(KV_GROUP)]

    s_next = scores(*units[0])
    heads = []
    for n, (j, hk) in enumerate(units):
        s_cur = s_next
        if n + 1 < len(units):
            s_next = scores(*units[n + 1])
        heads += attend(j, hk, s_cur)
        if hk == N_KV_HEADS - 1:
            attn = jnp.concatenate(heads, axis=0)
            inv = lax.rsqrt(jnp.mean(attn * attn, axis=0, keepdims=True) + NORM_EPS)
            o_ref[:, j * BLOCK:(j + 1) * BLOCK] = (attn * inv * g_ref[...]).astype(_BF16)
            heads = []


def _attn_call(sinks, q, kd, vt, g_attn, batch, seq):
    tq = TOK_TILE
    nt = seq // tq
    bpt = tq // BLOCK
    cur = lambda b, i, *_: (b * nt + i, 0)
    prev = lambda b, i, *_: (jnp.maximum((b * nt + i) * bpt - 1, 0), 0)
    cur_t = lambda b, i, *_: (0, b * nt + i)
    prev_t = lambda b, i, *_: (0, jnp.maximum((b * nt + i) * bpt - 1, 0))
    grid_spec = pltpu.PrefetchScalarGridSpec(
        num_scalar_prefetch=1,
        grid=(batch, nt),
        in_specs=[
            pl.BlockSpec((tq, ATTN_WIDTH), cur),
            pl.BlockSpec((tq, 2 * KV_COLS), cur),
            pl.BlockSpec((BLOCK, 2 * KV_COLS), prev),
            pl.BlockSpec((2 * KV_COLS, tq), cur_t),
            pl.BlockSpec((2 * KV_COLS, BLOCK), prev_t),
            pl.BlockSpec((ATTN_WIDTH, 1), lambda b, i, *_: (0, 0)),
        ],
        out_specs=pl.BlockSpec((ATTN_WIDTH, tq), cur_t),
        scratch_shapes=[pltpu.VMEM((2, 2 * BLOCK, BLOCK), _F32)],
    )
    return pl.pallas_call(
        _attn_kernel,
        grid_spec=grid_spec,
        out_shape=jax.ShapeDtypeStruct(q.shape[::-1], _BF16),
        compiler_params=pltpu.CompilerParams(dimension_semantics=("arbitrary", "arbitrary"),
                                             vmem_limit_bytes=VMEM_LIMIT),
        name="attn",
    )(sinks, q, kd, kd, vt, vt, g_attn)


def _dot_f32(a, b):
    return jnp.dot(a, b, preferred_element_type=_F32, precision=lax.Precision.HIGHEST)


def _shift_lanes(x, k):
    lo, hi = x[:, :LANES], x[:, LANES:]
    zero = jnp.zeros_like(lo)
    if k == 0:
        return x
    if k == LANES:
        return jnp.concatenate([zero, lo], axis=1)
    lane = lax.broadcasted_iota(jnp.int32, lo.shape, 1)
    if k < LANES:
        rl, rh = pltpu.roll(lo, k, 1), pltpu.roll(hi, k, 1)
        return jnp.concatenate([jnp.where(lane < k, zero, rl), jnp.where(lane < k, rl, rh)], axis=1)
    rl = pltpu.roll(lo, k - LANES, 1)
    return jnp.concatenate([zero, jnp.where(lane < k - LANES, zero, rl)], axis=1)


def _ssm_operators(col_ref, row_ref, bt_ref, ct_ref, dk_ref, m_ref, e_ref, f_ref):
    lr_c, li_c, dt_c = col_ref[0, 0], col_ref[0, 1], jnp.exp(col_ref[0, 2])
    lr_r, li_r, dt_r = row_ref[0, 0], row_ref[0, 1], jnp.exp(row_ref[0, 2])
    zr_c, zi_c, zr_r, zi_r = lr_c * dt_c, li_c * dt_c, lr_r * dt_r, li_r * dt_r

    tl = lax.broadcasted_iota(jnp.int32, (LANES, CHUNK_W), 1) // GROUP_CH
    ar_c, ai_c = jnp.exp(zr_c) * jnp.cos(zi_c), jnp.exp(zr_c) * jnp.sin(zi_c)
    sq_r, sq_i = ar_c, ai_c
    pr0 = jnp.where((tl & 1) == 1, sq_r, 1.0)
    pi0 = jnp.where((tl & 1) == 1, sq_i, 0.0)
    for bit in range(1, CHUNK.bit_length() - 1):
        sq_r, sq_i = sq_r * sq_r - sq_i * sq_i, 2.0 * sq_r * sq_i
        has = ((tl >> bit) & 1) == 1
        pr0, pi0 = (jnp.where(has, pr0 * sq_r - pi0 * sq_i, pr0), jnp.where(has, pr0 * sq_i + pi0 * sq_r, pi0))
    pr1, pi1 = pr0 * ar_c - pi0 * ai_c, pr0 * ai_c + pi0 * ar_c
    ctr, cti = ct_ref[0, 0], ct_ref[0, 1]
    w0r, w0i = pr0 * ctr - pi0 * cti, pr0 * cti + pi0 * ctr
    w1r, w1i = pr1 * ctr - pi1 * cti, pr1 * cti + pi1 * ctr
    top = lax.broadcasted_iota(jnp.int32, (LANES, CHUNK_W), 0) < SSM_STATE
    f_ref[0:LANES, 0:CHUNK_W] = jnp.where(top, w1r, 0.0)
    f_ref[0:LANES, CHUNK_W:] = jnp.where(top, 0.0, w1r)
    f_ref[LANES:, 0:CHUNK_W] = jnp.where(top, -w1i, 0.0)
    f_ref[LANES:, CHUNK_W:] = jnp.where(top, 0.0, -w1i)

    mag_r = jnp.exp(zr_r)
    ar_r, ai_r = mag_r * jnp.cos(zi_r), mag_r * jnp.sin(zi_r)
    den = lr_r * lr_r + li_r * li_r
    qr = ((ar_r - 1.0) * lr_r + ai_r * li_r) / den
    qi = (ai_r * lr_r - (ar_r - 1.0) * li_r) / den
    btr, bti = bt_ref[0, 0], bt_ref[0, 1]
    bbr, bbi = qr * btr - qi * bti, qr * bti + qi * btr
    n_row = lax.broadcasted_iota(jnp.int32, (CHUNK, LANES), 0).astype(_F32)
    mag_n = jnp.exp(zr_r * n_row)
    prn, pin = mag_n * jnp.cos(zi_r * n_row), mag_n * jnp.sin(zi_r * n_row)
    first = lax.broadcasted_iota(jnp.int32, (GROUP_CH, LANES), 1) < SSM_STATE
    diag = (lax.broadcasted_iota(jnp.int32, (CHUNK_W, CHUNK_W), 0)
            == lax.broadcasted_iota(jnp.int32, (CHUNK_W, CHUNK_W), 1))
    for a in range(2):
        mine = first if a == 0 else jnp.logical_not(first)
        br, bi = jnp.where(mine, bbr, 0.0), jnp.where(mine, bbi, 0.0)
        m0 = _dot_f32(br, w0r) - _dot_f32(bi, w0i)
        for s in range(CHUNK):
            m_ref[a, s * GROUP_CH:(s + 1) * GROUP_CH, :] = _shift_lanes(m0, s * GROUP_CH)
            pn_r, pn_i = prn[CHUNK - 1 - s:CHUNK - s, :], pin[CHUNK - 1 - s:CHUNK - s, :]
            rows = slice(a * CHUNK_W + s * GROUP_CH, a * CHUNK_W + (s + 1) * GROUP_CH)
            e_ref[rows, 0:LANES] = br * pn_r - bi * pn_i
            e_ref[rows, LANES:] = br * pn_i + bi * pn_r
        m_ref[a] = m_ref[a] + jnp.where(diag, dk_ref[0, a], 0.0)

    k_row = ((lax.broadcasted_iota(jnp.int32, (SUBLANES, LANES), 0) + 1) * CHUNK).astype(_F32)
    mag_k = jnp.exp(zr_r * k_row)
    return mag_k * jnp.cos(zi_r * k_row), mag_k * jnp.sin(zi_r * k_row)


def _ssm_kernel(u_ref, col_ref, row_ref, bt_ref, ct_ref, dk_ref, y_ref, m_ref, e_ref, f_ref, sr_ref, si_ref, xp_ref,
                *, chunks_per_seq):
    n_chunks = u_ref.shape[1]
    big_r, big_i = _ssm_operators(col_ref, row_ref, bt_ref, ct_ref, dk_ref, m_ref, e_ref, f_ref)
    u0, u1 = u_ref[0], u_ref[1]
    s = (_dot(u0, e_ref[0:CHUNK_W, :].astype(_BF16)) + _dot(u1, e_ref[CHUNK_W:, :].astype(_BF16)))
    sr_ref[...] = s[:, 0:LANES]
    si_ref[...] = s[:, LANES:2 * LANES]

    rows_per_seq = chunks_per_seq // SUBLANES
    sub = lax.broadcasted_iota(jnp.int32, (SUBLANES, LANES), 0)
    levels = []
    for d in (1, 2, 4):
        levels.append((d, jnp.where(sub >= d, jnp.broadcast_to(big_r[d - 1:d, :], (SUBLANES, LANES)), 0.0),
                       jnp.where(sub >= d, jnp.broadcast_to(big_i[d - 1:d, :], (SUBLANES, LANES)), 0.0)))

    def row_group(base, cre, cim):
        vr, vi = sr_ref[pl.ds(base, SUBLANES), :], si_ref[pl.ds(base, SUBLANES), :]
        for d, ar, ai in levels:
            wr, wi = pltpu.roll(vr, d, 0), pltpu.roll(vi, d, 0)
            vr, vi = vr + ar * wr - ai * wi, vi + ar * wi + ai * wr
        pr, pi = big_r, big_i
        xr = vr + pr * cre - pi * cim
        xi = vi + pr * cim + pi * cre
        xp_ref[pl.ds(base, SUBLANES), 0:LANES] = jnp.where(sub == 0, cre, pltpu.roll(xr, 1, 0))
        xp_ref[pl.ds(base, SUBLANES), LANES:2 * LANES] = jnp.where(sub == 0, cim, pltpu.roll(xi, 1, 0))
        return (jnp.broadcast_to(xr[SUBLANES - 1:SUBLANES, :], (SUBLANES, LANES)),
                jnp.broadcast_to(xi[SUBLANES - 1:SUBLANES, :], (SUBLANES, LANES)))

    n_seq = n_chunks // chunks_per_seq

    def step(r, carry):
        out = []
        for b in range(n_seq):
            base = pl.multiple_of((b * rows_per_seq + r) * SUBLANES, SUBLANES)
            out += row_group(base, carry[2 * b], carry[2 * b + 1])
        return tuple(out)

    zeros = jnp.zeros((SUBLANES, LANES), _F32)
    lax.fori_loop(0, rows_per_seq, step, (zeros,) * (2 * n_seq))

    xp = xp_ref[...].astype(_BF16)
    for a, ua in enumerate((u0, u1)):
        y = (_dot(ua, m_ref[a].astype(_BF16))
             + _dot(xp, f_ref[:, a * CHUNK_W:(a + 1) * CHUNK_W].astype(_BF16)))
        y_ref[a] = jax.nn.gelu(y).astype(_BF16)


def _ssm_call(u_g, lam_re, lam_im, log_step, b_re, b_im, c_re, c_im, d_skip, seq):
    n_chunks = u_g.shape[1]
    g, p_dim = lam_re.shape
    rows = jnp.stack([lam_re, lam_im, jnp.broadcast_to(log_step[:, None], (g, p_dim))], axis=0)
    rows = rows.reshape(3, g // 2, 1, 2 * p_dim).transpose(1, 0, 2, 3)
    cols = rows.transpose(0, 1, 3, 2)
    bt = jnp.stack([b_re, b_im], axis=0).reshape(2, g // 2, 2, p_dim, GROUP_CH)
    bt = bt.transpose(1, 0, 4, 2, 3).reshape(g // 2, 2, GROUP_CH, 2 * p_dim)
    ct = jnp.stack([c_re, c_im], axis=0).transpose(1, 0, 3, 2)
    ct = jnp.tile(ct, (1, 1, 1, CHUNK)).reshape(g // 2, 2, 2, p_dim, CHUNK_W)
    ct = ct.transpose(0, 2, 1, 3, 4).reshape(g // 2, 2, 2 * p_dim, CHUNK_W)
    dk = jnp.tile(d_skip, (1, CHUNK)).reshape(g // 2, 2, 1, CHUNK_W)
    pair3 = lambda i: (i, 0, 0)
    pair4 = lambda i: (i, 0, 0, 0)
    return pl.pallas_call(
        functools.partial(_ssm_kernel, chunks_per_seq=seq // CHUNK),
        grid=(N_PAIRS,),
        in_specs=[
            pl.BlockSpec((2, n_chunks, CHUNK_W), pair3),
            pl.BlockSpec((1, 3, 2 * p_dim, 1), pair4),
            pl.BlockSpec((1, 3, 1, 2 * p_dim), pair4),
            pl.BlockSpec((1, 2, GROUP_CH, 2 * p_dim), pair4),
            pl.BlockSpec((1, 2, 2 * p_dim, CHUNK_W), pair4),
            pl.BlockSpec((1, 2, 1, CHUNK_W), pair4),
        ],
        out_specs=pl.BlockSpec((2, n_chunks, CHUNK_W), pair3),
        out_shape=jax.ShapeDtypeStruct(u_g.shape, _BF16),
        scratch_shapes=[
            pltpu.VMEM((2, CHUNK_W, CHUNK_W), _F32),
            pltpu.VMEM((2 * CHUNK_W, 2 * LANES), _F32),
            pltpu.VMEM((2 * LANES, 2 * CHUNK_W), _F32),
            pltpu.VMEM((n_chunks, LANES), _F32),
            pltpu.VMEM((n_chunks, LANES), _F32),
            pltpu.VMEM((n_chunks, 2 * LANES), _F32),
        ],
        compiler_params=pltpu.CompilerParams(dimension_semantics=("arbitrary",), vmem_limit_bytes=VMEM_LIMIT),
        name="ssm",
    )(u_g, cols, rows, bt, ct, dk)


def _ffn_kernel(x_ref, mat_ref, y_ref, wg_ref, bg_ref, gs_ref, wo_ref, g2_ref, wu_ref, cw_ref, cb_ref, wd_ref, gf_ref,
                o_ref, carry_ref, act_ref, hn_ref, ys_ref, *, tiles_per_seq):
    tm = x_ref.shape[0]

    @pl.when((pl.program_id(0) % tiles_per_seq) == 0)
    def _():
        carry_ref[...] = jnp.zeros(carry_ref.shape, _F32)

    n_chunks = y_ref.shape[1]
    for j in range(N_SLABS):
        for h in range(CHUNK // GROUPS_PER_SLAB):
            src = [y_ref[j * GROUPS_PER_SLAB + gm, :, h * LANES:(h + 1) * LANES].astype(_F32)
                   for gm in range(GROUPS_PER_SLAB)]
            for tt, tile in enumerate(_transpose_lane_blocks(src)):
                ys_ref[j, pl.ds(h * GROUPS_PER_SLAB + tt, n_chunks, stride=CHUNK), :] = tile
    y = jnp.concatenate([ys_ref[j] for j in range(N_SLABS)], axis=1)
    z = _dot(y.astype(_BF16), wg_ref[...]) + bg_ref[...]
    mix_s = _rms(y * jax.nn.sigmoid(z), gs_ref[...]).astype(_BF16)

    attn_out = lax.dot_general(mat_ref[...], wo_ref[0:ATTN_WIDTH, :], (((0,), (0,)), ((), ())),
                               preferred_element_type=_F32)
    h1 = x_ref[...] + attn_out + _dot(mix_s, wo_ref[ATTN_WIDTH:, :])
    hn_ref[...] = _rms(h1, g2_ref[...]).astype(_BF16)

    def conv_part(c):
        cols = slice(c * FF_CHUNK, (c + 1) * FF_CHUNK)
        up = _dot(hn_ref[...], wu_ref[:, cols])
        ext = jnp.concatenate([carry_ref[:, cols], up], axis=0)
        carry_ref[:, cols] = up[tm - SUBLANES:, :]
        prev1 = pltpu.roll(ext, 1, 0)[SUBLANES:, :]
        prev2 = pltpu.roll(ext, 2, 0)[SUBLANES:, :]
        return (cw_ref[2:3, cols] * up + cw_ref[1:2, cols] * prev1 + cw_ref[0:1, cols] * prev2 + cb_ref[:, cols])

    for j in range(N_FF_CHUNKS):
        gate = conv_part(j)
        val = conv_part(j + N_FF_CHUNKS)
        act_ref[:, j * FF_CHUNK:(j + 1) * FF_CHUNK] = (jax.nn.silu(gate) * val).astype(_BF16)

    o_ref[...] = _rms(h1 + _dot(act_ref[...], wd_ref[...]), gf_ref[...])


def _ffn_call(x2, mix_a, y_g, w_glu, b_glu, g_ssm, w_out, ln2_g, w_up, conv_w, conv_b, w_down, lnf_g, seq):
    n = x2.shape[0]
    tm = FFN_TILE
    const = lambda i: (0, 0)
    once = pl.Buffered(1)
    return pl.pallas_call(
        functools.partial(_ffn_kernel, tiles_per_seq=seq // tm),
        grid=(n // tm,),
        in_specs=[
            pl.BlockSpec((tm, D_MODEL), lambda i: (i, 0)),
            pl.BlockSpec((ATTN_WIDTH, tm), lambda i: (0, i)),
            pl.BlockSpec((N_GROUPS, tm // CHUNK, CHUNK_W), lambda i: (0, i, 0)),
            pl.BlockSpec((SSM_WIDTH, SSM_WIDTH), const, pipeline_mode=once),
            pl.BlockSpec((1, SSM_WIDTH), const),
            pl.BlockSpec((1, SSM_WIDTH), const),
            pl.BlockSpec((D_MODEL, D_MODEL), const, pipeline_mode=once),
            pl.BlockSpec((1, D_MODEL), const),
            pl.BlockSpec((D_MODEL, 2 * D_FF), const, pipeline_mode=once),
            pl.BlockSpec((CONV_WIDTH, 2 * D_FF), const),
            pl.BlockSpec((1, 2 * D_FF), const),
            pl.BlockSpec((D_FF, D_MODEL), const, pipeline_mode=once),
            pl.BlockSpec((1, D_MODEL), const),
        ],
        out_specs=pl.BlockSpec((tm, D_MODEL), lambda i: (i, 0)),
        out_shape=jax.ShapeDtypeStruct((n, D_MODEL), _F32),
        scratch_shapes=[
            pltpu.VMEM((SUBLANES, 2 * D_FF), _F32),
            pltpu.VMEM((tm, D_FF), _BF16),
            pltpu.VMEM((tm, D_MODEL), _BF16),
            pltpu.VMEM((N_SLABS, tm, LANES), _F32),
        ],
        compiler_params=pltpu.CompilerParams(dimension_semantics=("arbitrary",), vmem_limit_bytes=VMEM_LIMIT),
        name="ffn",
    )(x2, mix_a, y_g, w_glu, b_glu, g_ssm, w_out, ln2_g, w_up, conv_w, conv_b, w_down, lnf_g)


def kernel(x, ln1_g, w_in, b_in, sinks, lam_re, lam_im, log_step, ssm_b_re, ssm_b_im, ssm_c_re, ssm_c_im, ssm_d,
           w_glu, b_glu, g_attn, g_ssm, w_out, ln2_g, w_up, conv_w, conv_b, w_down, lnf_g):
    batch, seq, _ = x.shape
    assert ln1_g.shape[0] == 1, "single-layer trunk"
    assert seq % TOK_TILE == 0 and seq % (CHUNK * SUBLANES) == 0
    n = batch * seq
    x2 = x.reshape(n, D_MODEL)

    wq, wk, wv, wu = (w_in[0][:, :ATTN_WIDTH], w_in[0][:, ATTN_WIDTH:ATTN_WIDTH + KV_COLS],
                      w_in[0][:, ATTN_WIDTH + KV_COLS:ATTN_WIDTH + 2 * KV_COLS], w_in[0][:, ATTN_WIDTH + 2 * KV_COLS:])
    bq, bk, bv, bu = (b_in[0][:ATTN_WIDTH], b_in[0][ATTN_WIDTH:ATTN_WIDTH + KV_COLS],
                      b_in[0][ATTN_WIDTH + KV_COLS:ATTN_WIDTH + 2 * KV_COLS], b_in[0][ATTN_WIDTH + 2 * KV_COLS:])
    dup = lambda t: jnp.concatenate([t[..., :HEAD_DIM], t[..., :HEAD_DIM], t[..., HEAD_DIM:], t[..., HEAD_DIM:]], axis=-1)
    w_ext = jnp.concatenate([wq, dup(wk), wu], axis=1).astype(_BF16)
    b_ext = jnp.concatenate([bq, dup(bk), bu])[None, :]
    zw, ones = jnp.zeros((D_MODEL, HEAD_DIM), _F32), jnp.ones((HEAD_DIM,), _F32)
    wvt = jnp.concatenate([wv[:, :HEAD_DIM], zw, wv[:, HEAD_DIM:], zw], axis=1).T.astype(_BF16)
    bvt = jnp.concatenate([bv[:HEAD_DIM], ones, bv[HEAD_DIM:], ones])[:, None]
    inv_freq = ROPE_THETA ** (-jnp.arange(ROT_HALF, dtype=_F32) * 2.0 / ROT_DIM)
    head_lane = jnp.arange(LANES) % HEAD_DIM
    freq = jnp.where(head_lane < ROT_DIM, inv_freq[head_lane % ROT_HALF], 0.0)[None, :]

    q, kd, vt, u_g = _proj_call(x2, ln1_g, w_ext, b_ext, wvt, bvt, freq, seq)
    mix_a = _attn_call(sinks[0] * math.log2(math.e), q, kd, vt, g_attn.T, batch, seq)

    y_g = _ssm_call(u_g, lam_re[0], lam_im[0], log_step[0], ssm_b_re[0], ssm_b_im[0], ssm_c_re[0], ssm_c_im[0],
                    ssm_d[0], seq)
    out = _ffn_call(x2, mix_a, y_g, w_glu[0].astype(_BF16), b_glu, g_ssm, w_out[0].astype(_BF16), ln2_g,
                    w_up[0].astype(_BF16), conv_w[0], conv_b, w_down[0].astype(_BF16), lnf_g[None, :], seq)
    return out.reshape(batch, seq, D_MODEL)
```

```python
import functools
import math

import jax
import jax.numpy as jnp
from jax import lax
from jax.experimental import pallas as pl
from jax.experimental.pallas import tpu as pltpu

D_MODEL = 1024
HEAD_DIM = 64
N_Q_HEADS = 8
N_KV_HEADS = 2
KV_GROUP = N_Q_HEADS // N_KV_HEADS
ATTN_WIDTH = N_Q_HEADS * HEAD_DIM
KV_COLS = N_KV_HEADS * HEAD_DIM
SSM_WIDTH = 512
WINDOW = 128
BLOCK = 128
ROPE_THETA = 500000.0
ROT_DIM = HEAD_DIM // 4
ROT_HALF = ROT_DIM // 2
GROUP_CH = 16
N_GROUPS = SSM_WIDTH // GROUP_CH
N_PAIRS = N_GROUPS // 2
SSM_STATE = 64
D_FF = 2816
CONV_WIDTH = 3
NORM_EPS = 1e-5
MASK_VALUE = -1e30

LANES = 128
SUBLANES = 8
CHUNK = 16
CHUNK_W = CHUNK * GROUP_CH
GROUPS_PER_SLAB = LANES // GROUP_CH
N_SLABS = SSM_WIDTH // LANES
FF_CHUNK = 256
N_FF_CHUNKS = D_FF // FF_CHUNK
VMEM_LIMIT = 56 * 1024 * 1024

TOK_TILE = 512
FFN_TILE = 512

_F32 = jnp.float32
_BF16 = jnp.bfloat16


def _rms(x, g):
    return x * lax.rsqrt(jnp.mean(x * x, axis=-1, keepdims=True) + NORM_EPS) * g


def _dot(a, b):
    return jnp.dot(a, b, preferred_element_type=_F32)


def _transpose_lane_blocks(tiles):
    assert len(tiles) == GROUPS_PER_SLAB
    blk = lax.broadcasted_iota(jnp.int32, tiles[0].shape, 1) // GROUP_CH
    for d in (4, 2, 1):
        keep = (blk & d) == 0
        shift = d * GROUP_CH
        nxt = list(tiles)
        for i in range(GROUPS_PER_SLAB):
            if i & d == 0:
                a, b = tiles[i], tiles[i + d]
                nxt[i] = jnp.where(keep, a, pltpu.roll(b, shift, 1))
                nxt[i + d] = jnp.where(keep, pltpu.roll(a, LANES - shift, 1), b)
        tiles = nxt
    return tiles


def _proj_kernel(x_ref, w_ref, b_ref, wvt_ref, bvt_ref, freq_ref, q_ref, k_ref, vt_ref, u_ref, us_ref, cs_ref,
                 *, tiles_per_seq):
    tm = x_ref.shape[0]
    freq = freq_ref[...]

    @pl.when(pl.program_id(0) == 0)
    def _():
        off = lax.broadcasted_iota(jnp.int32, (tm, LANES), 0).astype(_F32) * freq
        cs_ref[0] = jnp.cos(off)
        cs_ref[1] = jnp.sin(off)

    x = x_ref[...]
    hn = (x * lax.rsqrt(jnp.mean(x * x, axis=-1, keepdims=True) + NORM_EPS)).astype(_BF16)
    proj = _dot(hn, w_ref[...]) + b_ref[...]
    vt = lax.dot_general(wvt_ref[...], hn, (((1,), (1,)), ((), ())), preferred_element_type=_F32)
    vt_ref[...] = (vt + bvt_ref[...]).astype(_BF16)

    base = ((pl.program_id(0) % tiles_per_seq) * tm).astype(_F32) * freq
    cos_b, sin_b = jnp.cos(base), jnp.sin(base)
    hd = lax.broadcasted_iota(jnp.int32, (1, LANES), 1) % HEAD_DIM
    lo_dims, hi_dims = hd < ROT_HALF, (hd >= ROT_HALF) & (hd < ROT_DIM)
    cos_o, sin_o = cs_ref[0], cs_ref[1]
    cos = cos_b * cos_o - sin_b * sin_o
    sin_lo = jnp.where(lo_dims, -sin_b, 0.0) * cos_o + jnp.where(lo_dims, -cos_b, 0.0) * sin_o
    sin_hi = jnp.where(hi_dims, sin_b, 0.0) * cos_o + jnp.where(hi_dims, cos_b, 0.0) * sin_o

    def rot(blk):
        return blk * cos + pltpu.roll(blk, LANES - ROT_HALF, 1) * sin_lo + pltpu.roll(blk, ROT_HALF, 1) * sin_hi

    for c in range(ATTN_WIDTH // LANES):
        q_ref[:, c * LANES:(c + 1) * LANES] = rot(proj[:, c * LANES:(c + 1) * LANES]).astype(_BF16)
    k0 = ATTN_WIDTH
    for c in range(2 * KV_COLS // LANES):
        k_ref[:, c * LANES:(c + 1) * LANES] = rot(proj[:, k0 + c * LANES:k0 + (c + 1) * LANES]).astype(_BF16)

    u0 = k0 + 2 * KV_COLS
    for j in range(N_SLABS):
        us_ref[j] = proj[:, u0 + j * LANES:u0 + (j + 1) * LANES]
    n_chunks = tm // CHUNK
    for j in range(N_SLABS):
        for h in range(CHUNK // GROUPS_PER_SLAB):
            src = [us_ref[j, pl.ds(h * GROUPS_PER_SLAB + tt, n_chunks, stride=CHUNK), :]
                   for tt in range(GROUPS_PER_SLAB)]
            for gm, tile in enumerate(_transpose_lane_blocks(src)):
                u_ref[j * GROUPS_PER_SLAB + gm, :, h * LANES:(h + 1) * LANES] = tile.astype(_BF16)


def _proj_call(x2, w_ext, b_ext, wvt, bvt, freq, seq):
    n = x2.shape[0]
    tm = TOK_TILE
    nc = w_ext.shape[1]
    const = lambda i: (0, 0)
    return pl.pallas_call(
        functools.partial(_proj_kernel, tiles_per_seq=seq // tm),
        grid=(n // tm,),
        in_specs=[
            pl.BlockSpec((tm, D_MODEL), lambda i: (i, 0)),
            pl.BlockSpec((D_MODEL, nc), const),
            pl.BlockSpec((1, nc), const),
            pl.BlockSpec((2 * KV_COLS, D_MODEL), const),
            pl.BlockSpec((2 * KV_COLS, 1), const),
            pl.BlockSpec((1, LANES), const),
        ],
        out_specs=[
            pl.BlockSpec((tm, ATTN_WIDTH), lambda i: (i, 0)),
            pl.BlockSpec((tm, 2 * KV_COLS), lambda i: (i, 0)),
            pl.BlockSpec((2 * KV_COLS, tm), lambda i: (0, i)),
            pl.BlockSpec((N_GROUPS, tm // CHUNK, CHUNK_W), lambda i: (0, i, 0)),
        ],
        out_shape=[
            jax.ShapeDtypeStruct((n, ATTN_WIDTH), _BF16),
            jax.ShapeDtypeStruct((n, 2 * KV_COLS), _BF16),
            jax.ShapeDtypeStruct((2 * KV_COLS, n), _BF16),
            jax.ShapeDtypeStruct((N_GROUPS, n // CHUNK, CHUNK_W), _BF16),
        ],
        scratch_shapes=[pltpu.VMEM((N_SLABS, tm, LANES), _F32), pltpu.VMEM((2, tm, LANES), _F32)],
        compiler_params=pltpu.CompilerParams(dimension_semantics=("arbitrary",), vmem_limit_bytes=VMEM_LIMIT),
        name="proj",
    )(x2, w_ext, b_ext, wvt, bvt, freq)


def _attn_kernel(sink_ref, q_ref, kc_ref, kp_ref, vc_ref, vp_ref, g_ref, o_ref, cap_ref):
    tq = q_ref.shape[0]
    first_tile = pl.program_id(1) == 0
    lo = lax.broadcasted_iota(jnp.int32, (BLOCK, LANES), 1) < HEAD_DIM
    zero = jnp.zeros((BLOCK, LANES), _BF16)

    @pl.when((pl.program_id(0) == 0) & first_tile)
    def _():
        kj = lax.broadcasted_iota(jnp.int32, (2 * BLOCK, BLOCK), 0)
        qi = lax.broadcasted_iota(jnp.int32, (2 * BLOCK, BLOCK), 1)
        band = (kj > qi) & (kj <= qi + WINDOW)
        no_cap = float(jnp.finfo(jnp.float32).max)
        cap_ref[0] = jnp.where(band, no_cap, MASK_VALUE)
        cap_ref[1] = jnp.where(band & (kj >= BLOCK), no_cap, MASK_VALUE)

    units = [(j, hk) for j in range(tq // BLOCK) for hk in range(N_KV_HEADS)]

    def scores(j, hk):
        rows = slice(j * BLOCK, (j + 1) * BLOCK)
        kv = slice(hk * LANES, (hk + 1) * LANES)
        if j == 0:
            kcat = jnp.concatenate([kp_ref[:, kv], kc_ref[rows, kv]], axis=0)
        else:
            kcat = kc_ref[(j - 1) * BLOCK:(j + 1) * BLOCK, kv]
        qa = q_ref[rows, (2 * hk) * LANES:(2 * hk + 1) * LANES]
        qb = q_ref[rows, (2 * hk + 1) * LANES:(2 * hk + 2) * LANES]
        qs = jnp.concatenate([jnp.where(lo, qa, zero), jnp.where(lo, zero, qa),
                              jnp.where(lo, qb, zero), jnp.where(lo, zero, qb)], axis=0)
        return lax.dot_general(kcat, qs, (((1,), (1,)), ((), ())), preferred_element_type=_F32)

    def attend(j, hk, s):
        kv = slice(hk * LANES, (hk + 1) * LANES)
        cap = cap_ref[jnp.where(first_tile, 1, 0)] if j == 0 else cap_ref[0]
        if j == 0:
            vcat = jnp.concatenate([vp_ref[kv, :], vc_ref[kv, 0:BLOCK]], axis=1)
        else:
            vcat = vc_ref[kv, (j - 1) * BLOCK:(j + 1) * BLOCK]
        probs, rest = [], []
        for a in range(KV_GROUP):
            sa = jnp.minimum(s[:, a * BLOCK:(a + 1) * BLOCK], cap)
            sink = sink_ref[KV_GROUP * hk + a]
            m = jnp.maximum(jnp.max(sa, axis=0, keepdims=True), sink)
            probs.append(jnp.exp2(sa - m).astype(_BF16))
            rest.append(jnp.exp2(sink - m))
        o = _dot(vcat, jnp.concatenate(probs, axis=1))
        o = o[0:HEAD_DIM, :] / (o[HEAD_DIM:HEAD_DIM + 1, :] + jnp.concatenate(rest, axis=1))
        return [o[:, a * BLOCK:(a + 1) * BLOCK] for a in range(KV_GROUP)]

    s_next = scores(*units[0])
    heads = []
    for n, (j, hk) in enumerate(units):
        s_cur = s_next
        if n + 1 < len(units):
            s_next = scores(*units[n + 1])
        heads += attend(j, hk, s_cur)
        if hk == N_KV_HEADS - 1:
            attn = jnp.concatenate(heads, axis=0)
            inv = lax.rsqrt(jnp.mean(attn * attn, axis=0, keepdims=True) + NORM_EPS)
            o_ref[:, j * BLOCK:(j + 1) * BLOCK] = (attn * inv * g_ref[...]).astype(_BF16)
            heads = []


def _attn_call(sinks, q, kd, vt, g_attn, batch, seq):
    tq = TOK_TILE
    nt = seq // tq
    bpt = tq // BLOCK
    cur = lambda b, i, *_: (b * nt + i, 0)
    prev = lambda b, i, *_: (jnp.maximum((b * nt + i) * bpt - 1, 0), 0)
    cur_t = lambda b, i, *_: (0, b * nt + i)
    prev_t = lambda b, i, *_: (0, jnp.maximum((b * nt + i) * bpt - 1, 0))
    grid_spec = pltpu.PrefetchScalarGridSpec(
        num_scalar_prefetch=1,
        grid=(batch, nt),
        in_specs=[
            pl.BlockSpec((tq, ATTN_WIDTH), cur),
            pl.BlockSpec((tq, 2 * KV_COLS), cur),
            pl.BlockSpec((BLOCK, 2 * KV_COLS), prev),
            pl.BlockSpec((2 * KV_COLS, tq), cur_t),
            pl.BlockSpec((2 * KV_COLS, BLOCK), prev_t),
            pl.BlockSpec((ATTN_WIDTH, 1), lambda b, i, *_: (0, 0)),
        ],
        out_specs=pl.BlockSpec((ATTN_WIDTH, tq), cur_t),
        scratch_shapes=[pltpu.VMEM((2, 2 * BLOCK, BLOCK), _F32)],
    )
    return pl.pallas_call(
        _attn_kernel,
        grid_spec=grid_spec,
        out_shape=jax.ShapeDtypeStruct(q.shape[::-1], _BF16),
        compiler_params=pltpu.CompilerParams(dimension_semantics=("arbitrary", "arbitrary"),
                                             vmem_limit_bytes=VMEM_LIMIT),
        name="attn",
    )(sinks, q, kd, kd, vt, vt, g_attn)


def _dot_f32(a, b):
    return jnp.dot(a, b, preferred_element_type=_F32, precision=lax.Precision.HIGHEST)


def _shift_lanes(x, k):
    lo, hi = x[:, :LANES], x[:, LANES:]
    zero = jnp.zeros_like(lo)
    if k == 0:
        return x
    if k == LANES:
        return jnp.concatenate([zero, lo], axis=1)
    lane = lax.broadcasted_iota(jnp.int32, lo.shape, 1)
    if k < LANES:
        rl, rh = pltpu.roll(lo, k, 1), pltpu.roll(hi, k, 1)
        return jnp.concatenate([jnp.where(lane < k, zero, rl), jnp.where(lane < k, rl, rh)], axis=1)
    rl = pltpu.roll(lo, k - LANES, 1)
    return jnp.concatenate([zero, jnp.where(lane < k - LANES, zero, rl)], axis=1)


def _ssm_operators(col_ref, row_ref, bt_ref, ct_ref, dk_ref, m_ref, e_ref, f_ref):
    lr_c, li_c, dt_c = col_ref[0, 0], col_ref[0, 1], jnp.exp(col_ref[0, 2])
    lr_r, li_r, dt_r = row_ref[0, 0], row_ref[0, 1], jnp.exp(row_ref[0, 2])
    zr_c, zi_c, zr_r, zi_r = lr_c * dt_c, li_c * dt_c, lr_r * dt_r, li_r * dt_r

    tl = lax.broadcasted_iota(jnp.int32, (LANES, CHUNK_W), 1) // GROUP_CH
    ar_c, ai_c = jnp.exp(zr_c) * jnp.cos(zi_c), jnp.exp(zr_c) * jnp.sin(zi_c)
    sq_r, sq_i = ar_c, ai_c
    pr0 = jnp.where((tl & 1) == 1, sq_r, 1.0)
    pi0 = jnp.where((tl & 1) == 1, sq_i, 0.0)
    for bit in range(1, CHUNK.bit_length() - 1):
        sq_r, sq_i = sq_r * sq_r - sq_i * sq_i, 2.0 * sq_r * sq_i
        has = ((tl >> bit) & 1) == 1
        pr0, pi0 = (jnp.where(has, pr0 * sq_r - pi0 * sq_i, pr0), jnp.where(has, pr0 * sq_i + pi0 * sq_r, pi0))
    pr1, pi1 = pr0 * ar_c - pi0 * ai_c, pr0 * ai_c + pi0 * ar_c
    ctr, cti = ct_ref[0, 0], ct_ref[0, 1]
    w0r, w0i = pr0 * ctr - pi0 * cti, pr0 * cti + pi0 * ctr
    w1r, w1i = pr1 * ctr - pi1 * cti, pr1 * cti + pi1 * ctr
    top = lax.broadcasted_iota(jnp.int32, (LANES, CHUNK_W), 0) < SSM_STATE
    f_ref[0:LANES, 0:CHUNK_W] = jnp.where(top, w1r, 0.0)
    f_ref[0:LANES, CHUNK_W:] = jnp.where(top, 0.0, w1r)
    f_ref[LANES:, 0:CHUNK_W] = jnp.where(top, -w1i, 0.0)
    f_ref[LANES:, CHUNK_W:] = jnp.where(top, 0.0, -w1i)

    mag_r = jnp.exp(zr_r)
    ar_r, ai_r = mag_r * jnp.cos(zi_r), mag_r * jnp.sin(zi_r)
    den = lr_r * lr_r + li_r * li_r
    qr = ((ar_r - 1.0) * lr_r + ai_r * li_r) / den
    qi = (ai_r * lr_r - (ar_r - 1.0) * li_r) / den
    btr, bti = bt_ref[0, 0], bt_ref[0, 1]
    bbr, bbi = qr * btr - qi * bti, qr * bti + qi * btr
    n_row = lax.broadcasted_iota(jnp.int32, (CHUNK, LANES), 0).astype(_F32)
    mag_n = jnp.exp(zr_r * n_row)
    prn, pin = mag_n * jnp.cos(zi_r * n_row), mag_n * jnp.sin(zi_r * n_row)
    first = lax.broadcasted_iota(jnp.int32, (GROUP_CH, LANES), 1) < SSM_STATE
    diag = (lax.broadcasted_iota(jnp.int32, (CHUNK_W, CHUNK_W), 0)
            == lax.broadcasted_iota(jnp.int32, (CHUNK_W, CHUNK_W), 1))
    for a in range(2):
        mine = first if a == 0 else jnp.logical_not(first)
        br, bi = jnp.where(mine, bbr, 0.0), jnp.where(mine, bbi, 0.0)
        m0 = _dot_f32(br, w0r) - _dot_f32(bi, w0i)
        for s in range(CHUNK):
            m_ref[a, s * GROUP_CH:(s + 1) * GROUP_CH, :] = _shift_lanes(m0, s * GROUP_CH)
            pn_r, pn_i = prn[CHUNK - 1 - s:CHUNK - s, :], pin[CHUNK - 1 - s:CHUNK - s, :]
            rows = slice(a * CHUNK_W + s * GROUP_CH, a * CHUNK_W + (s + 1) * GROUP_CH)
            e_ref[rows, 0:LANES] = br * pn_r - bi * pn_i
            e_ref[rows, LANES:] = br * pn_i + bi * pn_r
        m_ref[a] = m_ref[a] + jnp.where(diag, dk_ref[0, a], 0.0)

    k_row = ((lax.broadcasted_iota(jnp.int32, (SUBLANES, LANES), 0) + 1) * CHUNK).astype(_F32)
    mag_k = jnp.exp(zr_r * k_row)
    return mag_k * jnp.cos(zi_r * k_row), mag_k * jnp.sin(zi_r * k_row)


def _ssm_kernel(u_ref, col_ref, row_ref, bt_ref, ct_ref, dk_ref, y_ref, m_ref, e_ref, f_ref, sr_ref, si_ref, xp_ref,
                *, chunks_per_seq):
    n_chunks = u_ref.shape[1]
    big_r, big_i = _ssm_operators(col_ref, row_ref, bt_ref, ct_ref, dk_ref, m_ref, e_ref, f_ref)
    u0, u1 = u_ref[0], u_ref[1]
    s = (_dot(u0, e_ref[0:CHUNK_W, :].astype(_BF16)) + _dot(u1, e_ref[CHUNK_W:, :].astype(_BF16)))
    sr_ref[...] = s[:, 0:LANES]
    si_ref[...] = s[:, LANES:2 * LANES]

    rows_per_seq = chunks_per_seq // SUBLANES
    sub = lax.broadcasted_iota(jnp.int32, (SUBLANES, LANES), 0)
    levels = []
    for d in (1, 2, 4):
        levels.append((d, jnp.where(sub >= d, jnp.broadcast_to(big_r[d - 1:d, :], (SUBLANES, LANES)), 0.0),
                       jnp.where(sub >= d, jnp.broadcast_to(big_i[d - 1:d, :], (SUBLANES, LANES)), 0.0)))

    def row_group(base, cre, cim):
        vr, vi = sr_ref[pl.ds(base, SUBLANES), :], si_ref[pl.ds(base, SUBLANES), :]
        for d, ar, ai in levels:
            wr, wi = pltpu.roll(vr, d, 0), pltpu.roll(vi, d, 0)
            vr, vi = vr + ar * wr - ai * wi, vi + ar * wi + ai * wr
        pr, pi = big_r, big_i
        xr = vr + pr * cre - pi * cim
        xi = vi + pr * cim + pi * cre
        xp_ref[pl.ds(base, SUBLANES), 0:LANES] = jnp.where(sub == 0, cre, pltpu.roll(xr, 1, 0))
        xp_ref[pl.ds(base, SUBLANES), LANES:2 * LANES] = jnp.where(sub == 0, cim, pltpu.roll(xi, 1, 0))
        return (jnp.broadcast_to(xr[SUBLANES - 1:SUBLANES, :], (SUBLANES, LANES)),
                jnp.broadcast_to(xi[SUBLANES - 1:SUBLANES, :], (SUBLANES, LANES)))

    n_seq = n_chunks // chunks_per_seq

    def step(r, carry):
        out = []
        for b in range(n_seq):
            base = pl.multiple_of((b * rows_per_seq + r) * SUBLANES, SUBLANES)
            out += row_group(base, carry[2 * b], carry[2 * b + 1])
        return tuple(out)

    zeros = jnp.zeros((SUBLANES, LANES), _F32)
    lax.fori_loop(0, rows_per_seq, step, (zeros,) * (2 * n_seq))

    xp = xp_ref[...].astype(_BF16)
    for a, ua in enumerate((u0, u1)):
        y = (_dot(ua, m_ref[a].astype(_BF16))
             + _dot(xp, f_ref[:, a * CHUNK_W:(a + 1) * CHUNK_W].astype(_BF16)))
        y_ref[a] = jax.nn.gelu(y).astype(_BF16)


def _ssm_call(u_g, lam_re, lam_im, log_step, b_re, b_im, c_re, c_im, d_skip, seq):
    n_chunks = u_g.shape[1]
    g, p_dim = lam_re.shape
    rows = jnp.stack([lam_re, lam_im, jnp.broadcast_to(log_step[:, None], (g, p_dim))], axis=0)
    rows = rows.reshape(3, g // 2, 1, 2 * p_dim).transpose(1, 0, 2, 3)
    cols = rows.transpose(0, 1, 3, 2)
    bt = jnp.stack([b_re, b_im], axis=0).reshape(2, g // 2, 2, p_dim, GROUP_CH)
    bt = bt.transpose(1, 0, 4, 2, 3).reshape(g // 2, 2, GROUP_CH, 2 * p_dim)
    ct = jnp.stack([c_re, c_im], axis=0).transpose(1, 0, 3, 2)
    ct = jnp.tile(ct, (1, 1, 1, CHUNK)).reshape(g // 2, 2, 2, p_dim, CHUNK_W)
    ct = ct.transpose(0, 2, 1, 3, 4).reshape(g // 2, 2, 2 * p_dim, CHUNK_W)
    dk = jnp.tile(d_skip, (1, CHUNK)).reshape(g // 2, 2, 1, CHUNK_W)
    pair3 = lambda i: (i, 0, 0)
    pair4 = lambda i: (i, 0, 0, 0)
    return pl.pallas_call(
        functools.partial(_ssm_kernel, chunks_per_seq=seq // CHUNK),
        grid=(N_PAIRS,),
        in_specs=[
            pl.BlockSpec((2, n_chunks, CHUNK_W), pair3),
            pl.BlockSpec((1, 3, 2 * p_dim, 1), pair4),
            pl.BlockSpec((1, 3, 1, 2 * p_dim), pair4),
            pl.BlockSpec((1, 2, GROUP_CH, 2 * p_dim), pair4),
            pl.BlockSpec((1, 2, 2 * p_dim, CHUNK_W), pair4),
            pl.BlockSpec((1, 2, 1, CHUNK_W), pair4),
        ],
        out_specs=pl.BlockSpec((2, n_chunks, CHUNK_W), pair3),
        out_shape=jax.ShapeDtypeStruct(u_g.shape, _BF16),
        scratch_shapes=[
            pltpu.VMEM((2, CHUNK_W, CHUNK_W), _F32),
            pltpu.VMEM((2 * CHUNK_W, 2 * LANES), _F32),
            pltpu.VMEM((2 * LANES, 2 * CHUNK_W), _F32),
            pltpu.VMEM((n_chunks, LANES), _F32),
            pltpu.VMEM((n_chunks, LANES), _F32),
            pltpu.VMEM((n_chunks, 2 * LANES), _F32),
        ],
        compiler_params=pltpu.CompilerParams(dimension_semantics=("arbitrary",), vmem_limit_bytes=VMEM_LIMIT),
        name="ssm",
    )(u_g, cols, rows, bt, ct, dk)


def _ffn_kernel(x_ref, mat_ref, y_ref, wg_ref, bg_ref, gs_ref, wo_ref, g2_ref, wu_ref, cw_ref, cb_ref, wd_ref, gf_ref,
                o_ref, carry_ref, act_ref, hn_ref, ys_ref, *, tiles_per_seq):
    tm = x_ref.shape[0]

    @pl.when((pl.program_id(0) % tiles_per_seq) == 0)
    def _():
        carry_ref[0:SUBLANES, :] = jnp.zeros((SUBLANES, carry_ref.shape[1]), _F32)

    attn_out = lax.dot_general(mat_ref[...], wo_ref[0:ATTN_WIDTH, :], (((0,), (0,)), ((), ())),
                               preferred_element_type=_F32)

    n_chunks = y_ref.shape[1]
    for j in range(N_SLABS):
        for h in range(CHUNK // GROUPS_PER_SLAB):
            src = [y_ref[j * GROUPS_PER_SLAB + gm, :, h * LANES:(h + 1) * LANES].astype(_F32)
                   for gm in range(GROUPS_PER_SLAB)]
            for tt, tile in enumerate(_transpose_lane_blocks(src)):
                ys_ref[j, pl.ds(h * GROUPS_PER_SLAB + tt, n_chunks, stride=CHUNK), :] = tile
    y = jnp.concatenate([ys_ref[j] for j in range(N_SLABS)], axis=1)
    z = _dot(y.astype(_BF16), wg_ref[...]) + bg_ref[...]
    mix_s = _rms(y * jax.nn.sigmoid(z), gs_ref[...]).astype(_BF16)

    h1 = x_ref[...] + attn_out + _dot(mix_s, wo_ref[ATTN_WIDTH:, :])
    hn_ref[...] = _rms(h1, g2_ref[...]).astype(_BF16)

    def conv_part(c):
        cols = slice(c * FF_CHUNK, (c + 1) * FF_CHUNK)
        up = _dot(hn_ref[...], wu_ref[:, cols])
        carry_ref[SUBLANES:, cols] = up
        prev1 = carry_ref[pl.ds(SUBLANES - 1, tm), cols]
        prev2 = carry_ref[pl.ds(SUBLANES - 2, tm), cols]
        out = cw_ref[2:3, cols] * up + cw_ref[1:2, cols] * prev1 + cw_ref[0:1, cols] * prev2 + cb_ref[:, cols]
        carry_ref[0:SUBLANES, cols] = up[tm - SUBLANES:, :]
        return out

    for j in range(N_FF_CHUNKS):
        gate = conv_part(j)
        val = conv_part(j + N_FF_CHUNKS)
        act_ref[:, j * FF_CHUNK:(j + 1) * FF_CHUNK] = (jax.nn.silu(gate) * val).astype(_BF16)

    o_ref[...] = _rms(h1 + _dot(act_ref[...], wd_ref[...]), gf_ref[...])


def _ffn_call(x2, mix_a, y_g, w_glu, b_glu, g_ssm, w_out, ln2_g, w_up, conv_w, conv_b, w_down, lnf_g, seq):
    n = x2.shape[0]
    tm = FFN_TILE
    const = lambda i: (0, 0)
    once = pl.Buffered(1)
    return pl.pallas_call(
        functools.partial(_ffn_kernel, tiles_per_seq=seq // tm),
        grid=(n // tm,),
        in_specs=[
            pl.BlockSpec((tm, D_MODEL), lambda i: (i, 0)),
            pl.BlockSpec((ATTN_WIDTH, tm), lambda i: (0, i)),
            pl.BlockSpec((N_GROUPS, tm // CHUNK, CHUNK_W), lambda i: (0, i, 0)),
            pl.BlockSpec((SSM_WIDTH, SSM_WIDTH), const, pipeline_mode=once),
            pl.BlockSpec((1, SSM_WIDTH), const),
            pl.BlockSpec((1, SSM_WIDTH), const),
            pl.BlockSpec((D_MODEL, D_MODEL), const, pipeline_mode=once),
            pl.BlockSpec((1, D_MODEL), const),
            pl.BlockSpec((D_MODEL, 2 * D_FF), const, pipeline_mode=once),
            pl.BlockSpec((CONV_WIDTH, 2 * D_FF), const),
            pl.BlockSpec((1, 2 * D_FF), const),
            pl.BlockSpec((D_FF, D_MODEL), const, pipeline_mode=once),
            pl.BlockSpec((1, D_MODEL), const),
        ],
        out_specs=pl.BlockSpec((tm, D_MODEL), lambda i: (i, 0)),
        out_shape=jax.ShapeDtypeStruct((n, D_MODEL), _F32),
        scratch_shapes=[
            pltpu.VMEM((SUBLANES + tm, 2 * D_FF), _F32),
            pltpu.VMEM((tm, D_FF), _BF16),
            pltpu.VMEM((tm, D_MODEL), _BF16),
            pltpu.VMEM((N_SLABS, tm, LANES), _F32),
        ],
        compiler_params=pltpu.CompilerParams(dimension_semantics=("arbitrary",), vmem_limit_bytes=VMEM_LIMIT),
        name="ffn",
    )(x2, mix_a, y_g, w_glu, b_glu, g_ssm, w_out, ln2_g, w_up, conv_w, conv_b, w_down, lnf_g)


def kernel(x, ln1_g, w_in, b_in, sinks, lam_re, lam_im, log_step, ssm_b_re, ssm_b_im, ssm_c_re, ssm_c_im, ssm_d,
           w_glu, b_glu, g_attn, g_ssm, w_out, ln2_g, w_up, conv_w, conv_b, w_down, lnf_g):
    batch, seq, _ = x.shape
    assert ln1_g.shape[0] == 1, "single-layer trunk"
    assert seq % TOK_TILE == 0 and seq % (CHUNK * SUBLANES) == 0
    n = batch * seq
    x2 = x.reshape(n, D_MODEL)

    wq, wk, wv, wu = (w_in[0][:, :ATTN_WIDTH], w_in[0][:, ATTN_WIDTH:ATTN_WIDTH + KV_COLS],
                      w_in[0][:, ATTN_WIDTH + KV_COLS:ATTN_WIDTH + 2 * KV_COLS], w_in[0][:, ATTN_WIDTH + 2 * KV_COLS:])
    bq, bk, bv, bu = (b_in[0][:ATTN_WIDTH], b_in[0][ATTN_WIDTH:ATTN_WIDTH + KV_COLS],
                      b_in[0][ATTN_WIDTH + KV_COLS:ATTN_WIDTH + 2 * KV_COLS], b_in[0][ATTN_WIDTH + 2 * KV_COLS:])
    dup = lambda t: jnp.concatenate([t[..., :HEAD_DIM], t[..., :HEAD_DIM], t[..., HEAD_DIM:], t[..., HEAD_DIM:]], axis=-1)
    scale = HEAD_DIM ** -0.5 * math.log2(math.e)
    gain = ln1_g[0][:, None]
    w_ext = (gain * jnp.concatenate([wq * scale, dup(wk), wu], axis=1)).astype(_BF16)
    b_ext = jnp.concatenate([bq * scale, dup(bk), bu])[None, :]
    zw, ones = jnp.zeros((D_MODEL, HEAD_DIM), _F32), jnp.ones((HEAD_DIM,), _F32)
    wvt = (gain * jnp.concatenate([wv[:, :HEAD_DIM], zw, wv[:, HEAD_DIM:], zw], axis=1)).T.astype(_BF16)
    bvt = jnp.concatenate([bv[:HEAD_DIM], ones, bv[HEAD_DIM:], ones])[:, None]
    inv_freq = ROPE_THETA ** (-jnp.arange(ROT_HALF, dtype=_F32) * 2.0 / ROT_DIM)
    head_lane = jnp.arange(LANES) % HEAD_DIM
    freq = jnp.where(head_lane < ROT_DIM, inv_freq[head_lane % ROT_HALF], 0.0)[None, :]

    q, kd, vt, u_g = _proj_call(x2, w_ext, b_ext, wvt, bvt, freq, seq)
    mix_a = _attn_call(sinks[0] * math.log2(math.e), q, kd, vt, g_attn.T, batch, seq)

    y_g = _ssm_call(u_g, lam_re[0], lam_im[0], log_step[0], ssm_b_re[0], ssm_b_im[0], ssm_c_re[0], ssm_c_im[0],
                    ssm_d[0], seq)
    out = _ffn_call(x2, mix_a, y_g, w_glu[0].astype(_BF16), b_glu, g_ssm, w_out[0].astype(_BF16), ln2_g,
                    w_up[0].astype(_BF16), conv_w[0], conv_b, w_down[0].astype(_BF16), lnf_g[None, :], seq)
    return out.reshape(batch, seq, D_MODEL)
```

```python
import functools
import math

import jax
import jax.numpy as jnp
from jax import lax
from jax.experimental import pallas as pl
from jax.experimental.pallas import tpu as pltpu

D_MODEL = 1024
HEAD_DIM = 64
N_Q_HEADS = 8
N_KV_HEADS = 2
KV_GROUP = N_Q_HEADS // N_KV_HEADS
ATTN_WIDTH = N_Q_HEADS * HEAD_DIM
KV_COLS = N_KV_HEADS * HEAD_DIM
SSM_WIDTH = 512
WINDOW = 128
BLOCK = 128
ROPE_THETA = 500000.0
ROT_DIM = HEAD_DIM // 4
ROT_HALF = ROT_DIM // 2
GROUP_CH = 16
N_GROUPS = SSM_WIDTH // GROUP_CH
N_PAIRS = N_GROUPS // 2
SSM_STATE = 64
D_FF = 2816
CONV_WIDTH = 3
NORM_EPS = 1e-5
MASK_VALUE = -1e30

LANES = 128
SUBLANES = 8
CHUNK = 16
CHUNK_W = CHUNK * GROUP_CH
GROUPS_PER_SLAB = LANES // GROUP_CH
N_SLABS = SSM_WIDTH // LANES
FF_CHUNK = 256
N_FF_CHUNKS = D_FF // FF_CHUNK
VMEM_LIMIT = 56 * 1024 * 1024

TOK_TILE = 1024
FFN_TILE = 1024

_F32 = jnp.float32
_BF16 = jnp.bfloat16


def _rms(x, g):
    return x * lax.rsqrt(jnp.mean(x * x, axis=-1, keepdims=True) + NORM_EPS) * g


def _dot(a, b):
    return jnp.dot(a, b, preferred_element_type=_F32)


def _transpose_lane_blocks(tiles):
    assert len(tiles) == GROUPS_PER_SLAB
    blk = lax.broadcasted_iota(jnp.int32, tiles[0].shape, 1) // GROUP_CH
    for d in (4, 2, 1):
        keep = (blk & d) == 0
        shift = d * GROUP_CH
        nxt = list(tiles)
        for i in range(GROUPS_PER_SLAB):
            if i & d == 0:
                a, b = tiles[i], tiles[i + d]
                nxt[i] = jnp.where(keep, a, pltpu.roll(b, shift, 1))
                nxt[i + d] = jnp.where(keep, pltpu.roll(a, LANES - shift, 1), b)
        tiles = nxt
    return tiles


def _proj_kernel(x_ref, g_ref, w_ref, b_ref, wvt_ref, bvt_ref, freq_ref, q_ref, k_ref, vt_ref, u_ref, us_ref, cs_ref,
                 *, tiles_per_seq):
    tm = x_ref.shape[0]
    freq = freq_ref[...]

    @pl.when(pl.program_id(0) == 0)
    def _():
        off = lax.broadcasted_iota(jnp.int32, (tm, LANES), 0).astype(_F32) * freq
        cs_ref[0] = jnp.cos(off)
        cs_ref[1] = jnp.sin(off)

    hn = _rms(x_ref[...], g_ref[...]).astype(_BF16)
    proj =_dot(hn, w_ref[...]) + b_ref[...]
    vt = lax.dot_general(wvt_ref[...], hn, (((1,), (1,)), ((), ())), preferred_element_type=_F32)
    vt_ref[...] = (vt + bvt_ref[...]).astype(_BF16)

    base = ((pl.program_id(0) % tiles_per_seq) * tm).astype(_F32) * freq
    cos_b, sin_b = jnp.cos(base), jnp.sin(base)
    hd = lax.broadcasted_iota(jnp.int32, (1, LANES), 1) % HEAD_DIM
    lo_dims, hi_dims = hd < ROT_HALF, (hd >= ROT_HALF) & (hd < ROT_DIM)
    cos_o, sin_o = cs_ref[0], cs_ref[1]
    cos = cos_b * cos_o - sin_b * sin_o
    sin_lo = jnp.where(lo_dims, -sin_b, 0.0) * cos_o + jnp.where(lo_dims, -cos_b, 0.0) * sin_o
    sin_hi = jnp.where(hi_dims, sin_b, 0.0) * cos_o + jnp.where(hi_dims, cos_b, 0.0) * sin_o

    def rot(blk):
        return blk * cos + pltpu.roll(blk, LANES - ROT_HALF, 1) * sin_lo + pltpu.roll(blk, ROT_HALF, 1) * sin_hi

    for c in range(ATTN_WIDTH // LANES):
        q_ref[:, c * LANES:(c + 1) * LANES] = rot(proj[:, c * LANES:(c + 1) * LANES]).astype(_BF16)
    k0 = ATTN_WIDTH
    for c in range(2 * KV_COLS // LANES):
        k_ref[:, c * LANES:(c + 1) * LANES] = rot(proj[:, k0 + c * LANES:k0 + (c + 1) * LANES]).astype(_BF16)

    u0 = k0 + 2 * KV_COLS
    for j in range(N_SLABS):
        us_ref[j] = proj[:, u0 + j * LANES:u0 + (j + 1) * LANES]
    n_chunks = tm // CHUNK
    for j in range(N_SLABS):
        for h in range(CHUNK // GROUPS_PER_SLAB):
            src = [us_ref[j, pl.ds(h * GROUPS_PER_SLAB + tt, n_chunks, stride=CHUNK), :]
                   for tt in range(GROUPS_PER_SLAB)]
            for gm, tile in enumerate(_transpose_lane_blocks(src)):
                u_ref[j * GROUPS_PER_SLAB + gm, :, h * LANES:(h + 1) * LANES] = tile.astype(_BF16)


def _proj_call(x2, ln1_g, w_ext, b_ext, wvt, bvt, freq, seq):
    n = x2.shape[0]
    tm = TOK_TILE
    nc = w_ext.shape[1]
    const = lambda i: (0, 0)
    return pl.pallas_call(
        functools.partial(_proj_kernel, tiles_per_seq=seq // tm),
        grid=(n // tm,),
        in_specs=[
            pl.BlockSpec((tm, D_MODEL), lambda i: (i, 0)),
            pl.BlockSpec((1, D_MODEL), const),
            pl.BlockSpec((D_MODEL, nc), const),
            pl.BlockSpec((1, nc), const),
            pl.BlockSpec((2 * KV_COLS, D_MODEL), const),
            pl.BlockSpec((2 * KV_COLS, 1), const),
            pl.BlockSpec((1, LANES), const),
        ],
        out_specs=[
            pl.BlockSpec((tm, ATTN_WIDTH), lambda i: (i, 0)),
            pl.BlockSpec((tm, 2 * KV_COLS), lambda i: (i, 0)),
            pl.BlockSpec((2 * KV_COLS, tm), lambda i: (0, i)),
            pl.BlockSpec((N_GROUPS, tm // CHUNK, CHUNK_W), lambda i: (0, i, 0)),
        ],
        out_shape=[
            jax.ShapeDtypeStruct((n, ATTN_WIDTH), _BF16),
            jax.ShapeDtypeStruct((n, 2 * KV_COLS), _BF16),
            jax.ShapeDtypeStruct((2 * KV_COLS, n), _BF16),
            jax.ShapeDtypeStruct((N_GROUPS, n // CHUNK, CHUNK_W), _BF16),
        ],
        scratch_shapes=[pltpu.VMEM((N_SLABS, tm, LANES), _F32), pltpu.VMEM((2, tm, LANES), _F32)],
        compiler_params=pltpu.CompilerParams(dimension_semantics=("arbitrary",), vmem_limit_bytes=VMEM_LIMIT),
        name="proj",
    )(x2, ln1_g, w_ext, b_ext, wvt, bvt, freq)


def _attn_kernel(sink_ref, q_ref, kc_ref, kp_ref, vc_ref, vp_ref, g_ref, o_ref, cap_ref):
    tq = q_ref.shape[0]
    first_tile = pl.program_id(1) == 0
    lo = lax.broadcasted_iota(jnp.int32, (BLOCK, LANES), 1) < HEAD_DIM
    zero = jnp.zeros((BLOCK, LANES), _BF16)

    @pl.when((pl.program_id(0) == 0) & first_tile)
    def _():
        kj = lax.broadcasted_iota(jnp.int32, (2 * BLOCK, BLOCK), 0)
        qi = lax.broadcasted_iota(jnp.int32, (2 * BLOCK, BLOCK), 1)
        band = (kj > qi) & (kj <= qi + WINDOW)
        no_cap = float(jnp.finfo(jnp.float32).max)
        cap_ref[0] = jnp.where(band, no_cap, MASK_VALUE)
        cap_ref[1] = jnp.where(band & (kj >= BLOCK), no_cap, MASK_VALUE)

    units = [(j, hk) for j in range(tq // BLOCK) for hk in range(N_KV_HEADS)]

    def scores(j, hk):
        rows = slice(j * BLOCK, (j + 1) * BLOCK)
        kv = slice(hk * LANES, (hk + 1) * LANES)
        if j == 0:
            kcat = jnp.concatenate([kp_ref[:, kv], kc_ref[rows, kv]], axis=0)
        else:
            kcat = kc_ref[(j - 1) * BLOCK:(j + 1) * BLOCK, kv]
        qa = q_ref[rows, (2 * hk) * LANES:(2 * hk + 1) * LANES]
        qb = q_ref[rows, (2 * hk + 1) * LANES:(2 * hk + 2) * LANES]
        qs = jnp.concatenate([jnp.where(lo, qa, zero), jnp.where(lo, zero, qa),
                              jnp.where(lo, qb, zero), jnp.where(lo, zero, qb)], axis=0)
        return lax.dot_general(kcat, qs, (((1,), (1,)), ((), ())), preferred_element_type=_F32)

    def attend(j, hk, s):
        kv = slice(hk * LANES, (hk + 1) * LANES)
        cap = cap_ref[jnp.where(first_tile, 1, 0)] if j == 0 else cap_ref[0]
        if j == 0:
            vcat = jnp.concatenate([vp_ref[kv, :], vc_ref[kv, 0:BLOCK]], axis=1)
        else:
            vcat = vc_ref[kv, (j - 1) * BLOCK:(j + 1) * BLOCK]
        probs, rest = [], []
        for a in range(KV_GROUP):
            sa = jnp.minimum(s[:, a * BLOCK:(a + 1) * BLOCK], cap)
            sink = sink_ref[KV_GROUP * hk + a]
            m = jnp.maximum(jnp.max(sa, axis=0, keepdims=True), sink)
            probs.append(jnp.exp2(sa - m).astype(_BF16))
            rest.append(jnp.exp2(sink - m))
        o = _dot(vcat, jnp.concatenate(probs, axis=1))
        o = o[0:HEAD_DIM, :] / (o[HEAD_DIM:HEAD_DIM + 1, :] + jnp.concatenate(rest, axis=1))
        return [o[:, a * BLOCK:(a + 1) * BLOCK] for a in range(KV_GROUP)]

    s_next = scores(*units[0])
    heads = []
    for n, (j, hk) in enumerate(units):
        s_cur = s_next
        if n + 1 < len(units):
            s_next = scores(*units[n + 1])
        heads += attend(j, hk, s_cur)
        if hk == N_KV_HEADS - 1:
            attn = jnp.concatenate(heads, axis=0)
            inv = lax.rsqrt(jnp.mean(attn * attn, axis=0, keepdims=True) + NORM_EPS)
            o_ref[:, j * BLOCK:(j + 1) * BLOCK] = (attn * inv * g_ref[...]).astype(_BF16)
            heads = []


def _attn_call(sinks, q, kd, vt, g_attn, batch, seq):
    tq = TOK_TILE
    nt = seq // tq
    bpt = tq // BLOCK
    cur = lambda b, i, *_: (b * nt + i, 0)
    prev = lambda b, i, *_: (jnp.maximum((b * nt + i) * bpt - 1, 0), 0)
    cur_t = lambda b, i, *_: (0, b * nt + i)
    prev_t = lambda b, i, *_: (0, jnp.maximum((b * nt + i) * bpt - 1, 0))
    grid_spec = pltpu.PrefetchScalarGridSpec(
        num_scalar_prefetch=1,
        grid=(batch, nt),
        in_specs=[
            pl.BlockSpec((tq, ATTN_WIDTH), cur),
            pl.BlockSpec((tq, 2 * KV_COLS), cur),
            pl.BlockSpec((BLOCK, 2 * KV_COLS), prev),
            pl.BlockSpec((2 * KV_COLS, tq), cur_t),
            pl.BlockSpec((2 * KV_COLS, BLOCK), prev_t),
            pl.BlockSpec((ATTN_WIDTH, 1), lambda b, i, *_: (0, 0)),
        ],
        out_specs=pl.BlockSpec((ATTN_WIDTH, tq), cur_t),
        scratch_shapes=[pltpu.VMEM((2, 2 * BLOCK, BLOCK), _F32)],
    )
    return pl.pallas_call(
        _attn_kernel,
        grid_spec=grid_spec,
        out_shape=jax.ShapeDtypeStruct(q.shape[::-1], _BF16),
        compiler_params=pltpu.CompilerParams(dimension_semantics=("arbitrary", "arbitrary"),
                                             vmem_limit_bytes=VMEM_LIMIT),
        name="attn",
    )(sinks, q, kd, kd, vt, vt, g_attn)


def _dot_f32(a, b):
    return jnp.dot(a, b, preferred_element_type=_F32, precision=lax.Precision.HIGHEST)


def _shift_lanes(x, k):
    lo, hi = x[:, :LANES], x[:, LANES:]
    zero = jnp.zeros_like(lo)
    if k == 0:
        return x
    if k == LANES:
        return jnp.concatenate([zero, lo], axis=1)
    lane = lax.broadcasted_iota(jnp.int32, lo.shape, 1)
    if k < LANES:
        rl, rh = pltpu.roll(lo, k, 1), pltpu.roll(hi, k, 1)
        return jnp.concatenate([jnp.where(lane < k, zero, rl), jnp.where(lane < k, rl, rh)], axis=1)
    rl = pltpu.roll(lo, k - LANES, 1)
    return jnp.concatenate([zero, jnp.where(lane < k - LANES, zero, rl)], axis=1)


def _ssm_operators(col_ref, row_ref, bt_ref, ct_ref, dk_ref, m_ref, e_ref, f_ref):
    lr_c, li_c, dt_c = col_ref[0, 0], col_ref[0, 1], jnp.exp(col_ref[0, 2])
    lr_r, li_r, dt_r = row_ref[0, 0], row_ref[0, 1], jnp.exp(row_ref[0, 2])
    zr_c, zi_c, zr_r, zi_r = lr_c * dt_c, li_c * dt_c, lr_r * dt_r, li_r * dt_r

    tl = lax.broadcasted_iota(jnp.int32, (LANES, CHUNK_W), 1) // GROUP_CH
    ar_c, ai_c = jnp.exp(zr_c) * jnp.cos(zi_c), jnp.exp(zr_c) * jnp.sin(zi_c)
    sq_r, sq_i = ar_c, ai_c
    pr0 = jnp.where((tl & 1) == 1, sq_r, 1.0)
    pi0 = jnp.where((tl & 1) == 1, sq_i, 0.0)
    for bit in range(1, CHUNK.bit_length() - 1):
        sq_r, sq_i = sq_r * sq_r - sq_i * sq_i, 2.0 * sq_r * sq_i
        has = ((tl >> bit) & 1) == 1
        pr0, pi0 = (jnp.where(has, pr0 * sq_r - pi0 * sq_i, pr0), jnp.where(has, pr0 * sq_i + pi0 * sq_r, pi0))
    pr1, pi1 = pr0 * ar_c - pi0 * ai_c, pr0 * ai_c + pi0 * ar_c
    ctr, cti = ct_ref[0, 0], ct_ref[0, 1]
    w0r, w0i = pr0 * ctr - pi0 * cti, pr0 * cti + pi0 * ctr
    w1r, w1i = pr1 * ctr - pi1 * cti, pr1 * cti + pi1 * ctr
    top = lax.broadcasted_iota(jnp.int32, (LANES, CHUNK_W), 0) < SSM_STATE
    f_ref[0:LANES, 0:CHUNK_W] = jnp.where(top, w1r, 0.0)
    f_ref[0:LANES, CHUNK_W:] = jnp.where(top, 0.0, w1r)
    f_ref[LANES:, 0:CHUNK_W] = jnp.where(top, -w1i, 0.0)
    f_ref[LANES:, CHUNK_W:] = jnp.where(top, 0.0, -w1i)

    mag_r = jnp.exp(zr_r)
    ar_r, ai_r = mag_r * jnp.cos(zi_r), mag_r * jnp.sin(zi_r)
    den = lr_r * lr_r + li_r * li_r
    qr = ((ar_r - 1.0) * lr_r + ai_r * li_r) / den
    qi = (ai_r * lr_r - (ar_r - 1.0) * li_r) / den
    btr, bti = bt_ref[0, 0], bt_ref[0, 1]
    bbr, bbi = qr * btr - qi * bti, qr * bti + qi * btr
    n_row = lax.broadcasted_iota(jnp.int32, (CHUNK, LANES), 0).astype(_F32)
    mag_n = jnp.exp(zr_r * n_row)
    prn, pin = mag_n * jnp.cos(zi_r * n_row), mag_n * jnp.sin(zi_r * n_row)
    first = lax.broadcasted_iota(jnp.int32, (GROUP_CH, LANES), 1) < SSM_STATE
    diag = (lax.broadcasted_iota(jnp.int32, (CHUNK_W, CHUNK_W), 0)
            == lax.broadcasted_iota(jnp.int32, (CHUNK_W, CHUNK_W), 1))
    for a in range(2):
        mine = first if a == 0 else jnp.logical_not(first)
        br, bi = jnp.where(mine, bbr, 0.0), jnp.where(mine, bbi, 0.0)
        m0 = _dot_f32(br, w0r) - _dot_f32(bi, w0i)
        for s in range(CHUNK):
            m_ref[a, s * GROUP_CH:(s + 1) * GROUP_CH, :] = _shift_lanes(m0, s * GROUP_CH)
            pn_r, pn_i = prn[CHUNK - 1 - s:CHUNK - s, :], pin[CHUNK - 1 - s:CHUNK - s, :]
            rows = slice(a * CHUNK_W + s * GROUP_CH, a * CHUNK_W + (s + 1) * GROUP_CH)
            e_ref[rows, 0:LANES] = br * pn_r - bi * pn_i
            e_ref[rows, LANES:] = br * pn_i + bi * pn_r
        m_ref[a] = m_ref[a] + jnp.where(diag, dk_ref[0, a], 0.0)

    k_row = ((lax.broadcasted_iota(jnp.int32, (SUBLANES, LANES), 0) + 1) * CHUNK).astype(_F32)
    mag_k = jnp.exp(zr_r * k_row)
    return mag_k * jnp.cos(zi_r * k_row), mag_k * jnp.sin(zi_r * k_row)


def _ssm_kernel(u_ref, col_ref, row_ref, bt_ref, ct_ref, dk_ref, y_ref, m_ref, e_ref, f_ref, sr_ref, si_ref, xp_ref,
                *, chunks_per_seq):
    n_chunks = u_ref.shape[1]
    big_r, big_i = _ssm_operators(col_ref, row_ref, bt_ref, ct_ref, dk_ref, m_ref, e_ref, f_ref)
    u0, u1 = u_ref[0], u_ref[1]
    s = (_dot(u0, e_ref[0:CHUNK_W, :].astype(_BF16)) + _dot(u1, e_ref[CHUNK_W:, :].astype(_BF16)))
    sr_ref[...] = s[:, 0:LANES]
    si_ref[...] = s[:, LANES:2 * LANES]

    rows_per_seq = chunks_per_seq // SUBLANES
    sub = lax.broadcasted_iota(jnp.int32, (SUBLANES, LANES), 0)
    levels = []
    for d in (1, 2, 4):
        levels.append((d, jnp.where(sub >= d, jnp.broadcast_to(big_r[d - 1:d, :], (SUBLANES, LANES)), 0.0),
                       jnp.where(sub >= d, jnp.broadcast_to(big_i[d - 1:d, :], (SUBLANES, LANES)), 0.0)))

    def row_group(base, cre, cim):
        vr, vi = sr_ref[pl.ds(base, SUBLANES), :], si_ref[pl.ds(base, SUBLANES), :]
        for d, ar, ai in levels:
            wr, wi = pltpu.roll(vr, d, 0), pltpu.roll(vi, d, 0)
            vr, vi = vr + ar * wr - ai * wi, vi + ar * wi + ai * wr
        pr, pi = big_r, big_i
        xr = vr + pr * cre - pi * cim
        xi = vi + pr * cim + pi * cre
        xp_ref[pl.ds(base, SUBLANES), 0:LANES] = jnp.where(sub == 0, cre, pltpu.roll(xr, 1, 0))
        xp_ref[pl.ds(base, SUBLANES), LANES:2 * LANES] = jnp.where(sub == 0, cim, pltpu.roll(xi, 1, 0))
        return (jnp.broadcast_to(xr[SUBLANES - 1:SUBLANES, :], (SUBLANES, LANES)),
                jnp.broadcast_to(xi[SUBLANES - 1:SUBLANES, :], (SUBLANES, LANES)))

    n_seq = n_chunks // chunks_per_seq

    def step(r, carry):
        out = []
        for b in range(n_seq):
            base = pl.multiple_of((b * rows_per_seq + r) * SUBLANES, SUBLANES)
            out += row_group(base, carry[2 * b], carry[2 * b + 1])
        return tuple(out)

    zeros = jnp.zeros((SUBLANES, LANES), _F32)
    lax.fori_loop(0, rows_per_seq, step, (zeros,) * (2 * n_seq))

    xp = xp_ref[...].astype(_BF16)
    for a, ua in enumerate((u0, u1)):
        y = (_dot(ua, m_ref[a].astype(_BF16))
             + _dot(xp, f_ref[:, a * CHUNK_W:(a + 1) * CHUNK_W].astype(_BF16)))
        y_ref[a] = jax.nn.gelu(y).astype(_BF16)


def _ssm_call(u_g, lam_re, lam_im, log_step, b_re, b_im, c_re, c_im, d_skip, seq):
    n_chunks = u_g.shape[1]
    g, p_dim = lam_re.shape
    rows = jnp.stack([lam_re, lam_im, jnp.broadcast_to(log_step[:, None], (g, p_dim))], axis=0)
    rows = rows.reshape(3, g // 2, 1, 2 * p_dim).transpose(1, 0, 2, 3)
    cols = rows.transpose(0, 1, 3, 2)
    bt = jnp.stack([b_re, b_im], axis=0).reshape(2, g // 2, 2, p_dim, GROUP_CH)
    bt = bt.transpose(1, 0, 4, 2, 3).reshape(g // 2, 2, GROUP_CH, 2 * p_dim)
    ct = jnp.stack([c_re, c_im], axis=0).transpose(1, 0, 3, 2)
    ct = jnp.tile(ct, (1, 1, 1, CHUNK)).reshape(g // 2, 2, 2, p_dim, CHUNK_W)
    ct = ct.transpose(0, 2, 1, 3, 4).reshape(g // 2, 2, 2 * p_dim, CHUNK_W)
    dk = jnp.tile(d_skip, (1, CHUNK)).reshape(g // 2, 2, 1, CHUNK_W)
    pair3 = lambda i: (i, 0, 0)
    pair4 = lambda i: (i, 0, 0, 0)
    return pl.pallas_call(
        functools.partial(_ssm_kernel, chunks_per_seq=seq // CHUNK),
        grid=(N_PAIRS,),
        in_specs=[
            pl.BlockSpec((2, n_chunks, CHUNK_W), pair3),
            pl.BlockSpec((1, 3, 2 * p_dim, 1), pair4),
            pl.BlockSpec((1, 3, 1, 2 * p_dim), pair4),
            pl.BlockSpec((1, 2, GROUP_CH, 2 * p_dim), pair4),
            pl.BlockSpec((1, 2, 2 * p_dim, CHUNK_W), pair4),
            pl.BlockSpec((1, 2, 1, CHUNK_W), pair4),
        ],
        out_specs=pl.BlockSpec((2, n_chunks, CHUNK_W), pair3),
        out_shape=jax.ShapeDtypeStruct(u_g.shape, _BF16),
        scratch_shapes=[
            pltpu.VMEM((2, CHUNK_W, CHUNK_W), _F32),
            pltpu.VMEM((2 * CHUNK_W, 2 * LANES), _F32),
            pltpu.VMEM((2 * LANES, 2 * CHUNK_W), _F32),
            pltpu.VMEM((n_chunks, LANES), _F32),
            pltpu.VMEM((n_chunks, LANES), _F32),
            pltpu.VMEM((n_chunks, 2 * LANES), _F32),
        ],
        compiler_params=pltpu.CompilerParams(dimension_semantics=("arbitrary",), vmem_limit_bytes=VMEM_LIMIT),
        name="ssm",
    )(u_g, cols, rows, bt, ct, dk)


def _ffn_kernel(x_ref, mat_ref, y_ref, wg_ref, bg_ref, gs_ref, wo_ref, g2_ref, wu_ref, cw_ref, cb_ref, wd_ref, gf_ref,
                o_ref, carry_ref, win_ref, act_ref, hn_ref, ys_ref, *, tiles_per_seq):
    tm = x_ref.shape[0]

    @pl.when((pl.program_id(0) % tiles_per_seq) == 0)
    def _():
        carry_ref[...] = jnp.zeros(carry_ref.shape, _F32)

    attn_out = lax.dot_general(mat_ref[...], wo_ref[0:ATTN_WIDTH, :], (((0,), (0,)), ((), ())),
                               preferred_element_type=_F32)

    n_chunks = y_ref.shape[1]
    for j in range(N_SLABS):
        for h in range(CHUNK // GROUPS_PER_SLAB):
            src = [y_ref[j * GROUPS_PER_SLAB + gm, :, h * LANES:(h + 1) * LANES].astype(_F32)
                   for gm in range(GROUPS_PER_SLAB)]
            for tt, tile in enumerate(_transpose_lane_blocks(src)):
                ys_ref[j, pl.ds(h * GROUPS_PER_SLAB + tt, n_chunks, stride=CHUNK), :] = tile
    y = jnp.concatenate([ys_ref[j] for j in range(N_SLABS)], axis=1)
    z = _dot(y.astype(_BF16), wg_ref[...]) + bg_ref[...]
    mix_s = _rms(y * jax.nn.sigmoid(z), gs_ref[...]).astype(_BF16)

    h1 = x_ref[...] + attn_out + _dot(mix_s, wo_ref[ATTN_WIDTH:, :])
    hn_ref[...] = _rms(h1, g2_ref[...]).astype(_BF16)

    def conv_part(c):
        cols = slice(c * FF_CHUNK, (c + 1) * FF_CHUNK)
        win = win_ref.at[c // N_FF_CHUNKS]
        up = _dot(hn_ref[...], wu_ref[:, cols])
        win[0:SUBLANES, :] = carry_ref[:, cols]
        win[SUBLANES:, :] = up
        carry_ref[:, cols] = up[tm - SUBLANES:, :]
        prev1 = win[pl.ds(SUBLANES - 1, tm), :]
        prev2 = win[pl.ds(SUBLANES - 2, tm), :]
        return cw_ref[2:3, cols] * up + cw_ref[1:2, cols] * prev1 + cw_ref[0:1, cols] * prev2 + cb_ref[:, cols]

    for j in range(N_FF_CHUNKS):
        gate = conv_part(j)
        val = conv_part(j + N_FF_CHUNKS)
        act_ref[:, j * FF_CHUNK:(j + 1) * FF_CHUNK] = (jax.nn.silu(gate) * val).astype(_BF16)

    o_ref[...] = _rms(h1 + _dot(act_ref[...], wd_ref[...]), gf_ref[...])


def _ffn_call(x2, mix_a, y_g, w_glu, b_glu, g_ssm, w_out, ln2_g, w_up, conv_w, conv_b, w_down, lnf_g, seq):
    n = x2.shape[0]
    tm = FFN_TILE
    const = lambda i: (0, 0)
    once = pl.Buffered(1)
    return pl.pallas_call(
        functools.partial(_ffn_kernel, tiles_per_seq=seq // tm),
        grid=(n // tm,),
        in_specs=[
            pl.BlockSpec((tm, D_MODEL), lambda i: (i, 0)),
            pl.BlockSpec((ATTN_WIDTH, tm), lambda i: (0, i)),
            pl.BlockSpec((N_GROUPS, tm // CHUNK, CHUNK_W), lambda i: (0, i, 0)),
            pl.BlockSpec((SSM_WIDTH, SSM_WIDTH), const, pipeline_mode=once),
            pl.BlockSpec((1, SSM_WIDTH), const),
            pl.BlockSpec((1, SSM_WIDTH), const),
            pl.BlockSpec((D_MODEL, D_MODEL), const, pipeline_mode=once),
            pl.BlockSpec((1, D_MODEL), const),
            pl.BlockSpec((D_MODEL, 2 * D_FF), const, pipeline_mode=once),
            pl.BlockSpec((CONV_WIDTH, 2 * D_FF), const),
            pl.BlockSpec((1, 2 * D_FF), const),
            pl.BlockSpec((D_FF, D_MODEL), const, pipeline_mode=once),
            pl.BlockSpec((1, D_MODEL), const),
        ],
        out_specs=pl.BlockSpec((tm, D_MODEL), lambda i: (i, 0)),
        out_shape=jax.ShapeDtypeStruct((n, D_MODEL), _F32),
        scratch_shapes=[
            pltpu.VMEM((SUBLANES, 2 * D_FF), _F32),
            pltpu.VMEM((2, SUBLANES + tm, FF_CHUNK), _F32),
            pltpu.VMEM((tm, D_FF), _BF16),
            pltpu.VMEM((tm, D_MODEL), _BF16),
            pltpu.VMEM((N_SLABS, tm, LANES), _F32),
        ],
        compiler_params=pltpu.CompilerParams(dimension_semantics=("arbitrary",), vmem_limit_bytes=VMEM_LIMIT),
        name="ffn",
    )(x2, mix_a, y_g, w_glu, b_glu, g_ssm, w_out, ln2_g, w_up, conv_w, conv_b, w_down, lnf_g)


def kernel(x, ln1_g, w_in, b_in, sinks, lam_re, lam_im, log_step, ssm_b_re, ssm_b_im, ssm_c_re, ssm_c_im, ssm_d,
           w_glu, b_glu, g_attn, g_ssm, w_out, ln2_g, w_up, conv_w, conv_b, w_down, lnf_g):
    batch, seq, _ = x.shape
    assert ln1_g.shape[0] == 1, "single-layer trunk"
    assert seq % TOK_TILE == 0 and seq % (CHUNK * SUBLANES) == 0
    n = batch * seq
    x2 = x.reshape(n, D_MODEL)

    wq, wk, wv, wu = (w_in[0][:, :ATTN_WIDTH], w_in[0][:, ATTN_WIDTH:ATTN_WIDTH + KV_COLS],
                      w_in[0][:, ATTN_WIDTH + KV_COLS:ATTN_WIDTH + 2 * KV_COLS], w_in[0][:, ATTN_WIDTH + 2 * KV_COLS:])
    bq, bk, bv, bu = (b_in[0][:ATTN_WIDTH], b_in[0][ATTN_WIDTH:ATTN_WIDTH + KV_COLS],
                      b_in[0][ATTN_WIDTH + KV_COLS:ATTN_WIDTH + 2 * KV_COLS], b_in[0][ATTN_WIDTH + 2 * KV_COLS:])
    dup = lambda t: jnp.concatenate([t[..., :HEAD_DIM], t[..., :HEAD_DIM], t[..., HEAD_DIM:], t[..., HEAD_DIM:]], axis=-1)
    scale = HEAD_DIM ** -0.5 * math.log2(math.e)
    w_ext = jnp.concatenate([wq * scale, dup(wk), wu], axis=1).astype(_BF16)
    b_ext = jnp.concatenate([bq * scale, dup(bk), bu])[None, :]
    zw, ones = jnp.zeros((D_MODEL, HEAD_DIM), _F32), jnp.ones((HEAD_DIM,), _F32)
    wvt = jnp.concatenate([wv[:, :HEAD_DIM], zw, wv[:, HEAD_DIM:], zw], axis=1).T.astype(_BF16)
    bvt = jnp.concatenate([bv[:HEAD_DIM], ones, bv[HEAD_DIM:], ones])[:, None]
    inv_freq = ROPE_THETA ** (-jnp.arange(ROT_HALF, dtype=_F32) * 2.0 / ROT_DIM)
    head_lane = jnp.arange(LANES) % HEAD_DIM
    freq = jnp.where(head_lane < ROT_DIM, inv_freq[head_lane % ROT_HALF], 0.0)[None, :]

    q, kd, vt, u_g = _proj_call(x2, ln1_g, w_ext, b_ext, wvt, bvt, freq, seq)
    mix_a = _attn_call(sinks[0] * math.log2(math.e), q, kd, vt, g_attn.T, batch, seq)

    y_g = _ssm_call(u_g, lam_re[0], lam_im[0], log_step[0], ssm_b_re[0], ssm_b_im[0], ssm_c_re[0], ssm_c_im[0],
                    ssm_d[0], seq)
    out = _ffn_call(x2, mix_a, y_g, w_glu[0].astype(_BF16), b_glu, g_ssm, w_out[0].astype(_BF16), ln2_g,
                    w_up[0].astype(_BF16), conv_w[0], conv_b, w_down[0].astype(_BF16), lnf_g[None, :], seq)
    return out.reshape(batch, seq, D_MODEL)
```
